```python
import jax
import jax.numpy as jnp
from jax import lax
import numpy as np

D_MODEL = 1024
BATCH = 8
SEQ = 4096
DEPTH = 4

N_HEADS = 16
HEAD_DIM = D_MODEL // N_HEADS
N_MIXERS = 3
RMS_EPS = 1e-6

NSA_KV_GROUPS = 4
NSA_HPG = N_HEADS // NSA_KV_GROUPS
NSA_KV_DIM = NSA_KV_GROUPS * HEAD_DIM
NSA_IN_DIM = N_HEADS * HEAD_DIM + 6 * NSA_KV_DIM + 3 * N_HEADS
CMP_BLOCK = 32
CMP_STRIDE = 16
CMP_RATIO = CMP_BLOCK // CMP_STRIDE
CMP_HIDDEN = 2 * HEAD_DIM
SEL_BLOCK = 64
SEL_TOPN = 16
WINDOW = 512
NSA_Q_BLOCK = 32

MOBA_BLOCK = 256
MOBA_TOPK = 3
MOBA_Q_BLOCK = 16

SB_Q_BLOCK = 128

D_FF = 3584
N_EXPERTS = 8
MOE_TOPK = 2

kernel_name = 'hybrid_nsa_moba_stickbreaking_moe'


def rms_norm(x, g):
    xf = x.astype(jnp.float32)
    y = xf * lax.rsqrt(jnp.mean(xf * xf, axis=-1, keepdims=True) + RMS_EPS)
    return (y * g.astype(jnp.float32)).astype(x.dtype)


def alibi_slopes():
    return 2.0 ** (-8.0 * jnp.arange(1, N_HEADS + 1, dtype=jnp.float32) / N_HEADS)


def masked_softmax(s, mask):
    s = jnp.where(mask, s, -jnp.inf)
    m = jnp.max(s, axis=-1, keepdims=True)
    m = jnp.where(jnp.isfinite(m), m, 0.0)
    p = jnp.exp(s - m)
    return p / jnp.maximum(jnp.sum(p, axis=-1, keepdims=True), 1e-30)


def sweep_query_blocks(fn, n_blocks, B, S):
    out = lax.map(fn, jnp.arange(n_blocks))
    return out.transpose(1, 0, 2, 3).reshape(B, S, -1)


def compress_blocks(k, pos, w1, w2):
    B, S = k.shape[:2]
    n_chunks = S // CMP_STRIDE
    n_cmp = n_chunks - CMP_RATIO + 1
    chunks = k.reshape(B, n_chunks, CMP_STRIDE, NSA_KV_GROUPS, HEAD_DIM)
    hid = []
    for r in range(CMP_RATIO):
        rows = slice(r * CMP_STRIDE, (r + 1) * CMP_STRIDE)
        seg = chunks[:, r:r + n_cmp] + pos[rows][:, None, :]
        hid.append(jnp.einsum('bnlgd,ldh->bngh', seg, w1[rows]))
    return jnp.einsum('bngh,hd->bngd', jax.nn.gelu(sum(hid)), w2)


def nsa_mixer(h, w_in, cmpk_pos, cmpk_w1, cmpk_w2, cmpv_pos, cmpv_w1, cmpv_w2, w_out):
    B, S, D = h.shape
    G, P, Dh, T = NSA_KV_GROUPS, NSA_HPG, HEAD_DIM, NSA_Q_BLOCK
    splits = [N_HEADS * Dh + i * NSA_KV_DIM for i in range(7)]
    q, kc, vc, ks, vs, kw, vw, g_logit = jnp.split(h @ w_in, splits, axis=-1)
    q = q.reshape(B, S, G, P, Dh)
    kc, vc, ks, vs, kw, vw = (a.reshape(B, S, G, Dh) for a in (kc, vc, ks, vs, kw, vw))
    gates = jax.nn.sigmoid(g_logit.astype(jnp.float32)).reshape(B, S, 3, G, P)
    slopes = alibi_slopes().reshape(G, P)[None, :, :, None, None]
    scale = Dh ** -0.5

    k_cmp = compress_blocks(kc, cmpk_pos, cmpk_w1, cmpk_w2)
    v_cmp = compress_blocks(vc, cmpv_pos, cmpv_w1, cmpv_w2)
    n_cmp = k_cmp.shape[1]
    cmp_start = jnp.arange(n_cmp) * CMP_STRIDE
    cmp_end = cmp_start + (CMP_BLOCK - 1)
    cmp_centre = cmp_start.astype(jnp.float32) + 0.5 * (CMP_BLOCK - 1)

    n_sel = S // SEL_BLOCK
    sel_j = jnp.arange(n_sel)
    overlap = ((cmp_start[:, None] < (sel_j[None, :] + 1) * SEL_BLOCK)
               & (cmp_end[:, None] >= sel_j[None, :] * SEL_BLOCK)).astype(jnp.float32)
    top_n = min(SEL_TOPN, n_sel)
    k_sel_len = top_n * SEL_BLOCK
    ks_blk = ks.reshape(B, n_sel, SEL_BLOCK, G, Dh).transpose(0, 3, 1, 2, 4)
    vs_blk = vs.reshape(B, n_sel, SEL_BLOCK, G, Dh).transpose(0, 3, 1, 2, 4)
    b_idx = jnp.arange(B)[:, None, None, None]
    g_idx = jnp.arange(G)[None, :, None, None]

    pad = ((0, 0), (WINDOW, 0), (0, 0), (0, 0))
    kw_pad = jnp.pad(kw, pad)
    vw_pad = jnp.pad(vw, pad)

    def block(i):
        t0 = i * T
        t = t0 + jnp.arange(T)
        qb = lax.dynamic_slice_in_dim(q, t0, T, axis=1)
        gb = lax.dynamic_slice_in_dim(gates, t0, T, axis=1)

        s_c = jnp.einsum('btgpd,bngd->bgptn', qb, k_cmp).astype(jnp.float32) * scale
        s_c = s_c - slopes * (t[:, None] - cmp_centre[None, :])
        p_c = masked_softmax(s_c, cmp_end[None, :] <= t[:, None])
        o_c = jnp.einsum('bgptn,bngd->btgpd', p_c.astype(v_cmp.dtype), v_cmp)

        imp = jnp.einsum('bgptn,nj->bgtj', p_c, overlap)
        cur = t // SEL_BLOCK
        forced = (sel_j[None, :] == 0) | (sel_j[None, :] == cur[:, None]) | (sel_j[None, :] == cur[:, None] - 1)
        future = sel_j[None, :] > cur[:, None]
        imp = jnp.where(future, -jnp.inf, jnp.where(forced, jnp.inf, imp))
        _, idx = lax.top_k(imp, top_n)
        k_g = ks_blk[b_idx, g_idx, idx].reshape(B, G, T, k_sel_len, Dh)
        v_g = vs_blk[b_idx, g_idx, idx].reshape(B, G, T, k_sel_len, Dh)
        pos_g = (idx[..., None] * SEL_BLOCK + jnp.arange(SEL_BLOCK)).reshape(B, G, T, k_sel_len)
        dist_g = (t[:, None] - pos_g)[:, :, None]
        s_s = jnp.einsum('btgpd,bgtkd->bgptk', qb, k_g).astype(jnp.float32) * scale - slopes * dist_g
        p_s = masked_softmax(s_s, dist_g >= 0)
        o_s = jnp.einsum('bgptk,bgtkd->btgpd', p_s.astype(v_g.dtype), v_g)

        kwb = lax.dynamic_slice_in_dim(kw_pad, t0, WINDOW + T, axis=1)
        vwb = lax.dynamic_slice_in_dim(vw_pad, t0, WINDOW + T, axis=1)
        pos_w = t0 - WINDOW + jnp.arange(WINDOW + T)
        dist_w = t[:, None] - pos_w[None, :]
        mask_w = (dist_w >= 0) & (dist_w < WINDOW) & (pos_w[None, :] >= 0)
        s_w = jnp.einsum('btgpd,bsgd->bgpts', qb, kwb).astype(jnp.float32) * scale - slopes * dist_w
        p_w = masked_softmax(s_w, mask_w)
        o_w = jnp.einsum('bgpts,bsgd->btgpd', p_w.astype(vwb.dtype), vwb)

        out = (gb[:, :, 0, :, :, None] * o_c + gb[:, :, 1, :, :, None] * o_s
               + gb[:, :, 2, :, :, None] * o_w)
        return out.reshape(B, T, D).astype(h.dtype)

    o = sweep_query_blocks(block, S // T, B, S)
    return o @ w_out


def moba_mixer(h, w_in, w_out):
    B, S, D = h.shape
    H, Dh, T, L = N_HEADS, HEAD_DIM, MOBA_Q_BLOCK, MOBA_BLOCK
    q, k, v = jnp.split(h @ w_in, 3, axis=-1)
    q, k, v = (a.reshape(B, S, H, Dh) for a in (q, k, v))
    n_blk = -(-S // L)
    pad = ((0, 0), (0, n_blk * L - S), (0, 0), (0, 0))
    kp = jnp.pad(k, pad)
    vp = jnp.pad(v, pad)
    k_blk = kp.reshape(B, n_blk, L, H, Dh).transpose(0, 3, 1, 2, 4)
    v_blk = vp.reshape(B, n_blk, L, H, Dh).transpose(0, 3, 1, 2, 4)
    k_mean = jnp.mean(k_blk.astype(jnp.float32), axis=3)
    top = min(MOBA_TOPK, n_blk)
    blk_j = jnp.arange(n_blk)
    slopes = alibi_slopes()[None, :, None, None]
    scale = Dh ** -0.5
    b_idx = jnp.arange(B)[:, None, None, None]
    h_idx = jnp.arange(H)[None, :, None, None]

    def block(i):
        t0 = i * T
        t = t0 + jnp.arange(T)
        cb = t0 // L
        qb = lax.dynamic_slice_in_dim(q, t0, T, axis=1)
        gate = jnp.einsum('bthd,bhnd->bhtn', qb.astype(jnp.float32), k_mean)
        gate = jnp.where(blk_j < cb, gate, -jnp.inf)
        _, idx = lax.top_k(gate, top)
        k_g = k_blk[b_idx, h_idx, idx].reshape(B, H, T, top * L, Dh)
        v_g = v_blk[b_idx, h_idx, idx].reshape(B, H, T, top * L, Dh)
        pos_g = (idx[..., None] * L + jnp.arange(L)).reshape(B, H, T, top * L)
        s_g = (jnp.einsum('bthd,bhtkd->bhtk', qb, k_g).astype(jnp.float32) * scale
               - slopes * (t[:, None] - pos_g))
        mask_g = pos_g < cb * L
        ko = lax.dynamic_slice_in_dim(kp, cb * L, L, axis=1)
        vo = lax.dynamic_slice_in_dim(vp, cb * L, L, axis=1)
        dist_o = t[:, None] - (cb * L + jnp.arange(L))[None, :]
        s_o = jnp.einsum('bthd,bshd->bhts', qb, ko).astype(jnp.float32) * scale - slopes * dist_o
        mask_o = jnp.broadcast_to(dist_o >= 0, s_o.shape)
        p = masked_softmax(jnp.concatenate([s_g, s_o], axis=-1),
                           jnp.concatenate([mask_g, mask_o], axis=-1))
        p_g = p[..., :top * L].astype(v.dtype)
        p_o = p[..., top * L:].astype(v.dtype)
        out = jnp.einsum('bhtk,bhtkd->bthd', p_g, v_g) + jnp.einsum('bhts,bshd->bthd', p_o, vo)
        return out.reshape(B, T, D)

    o = sweep_query_blocks(block, S // T, B, S)
    return o @ w_out


def stick_breaking_mixer(h, w_in, w_out):
    B, S, D = h.shape
    H, Dh, T = N_HEADS, HEAD_DIM, SB_Q_BLOCK
    q, k, v = jnp.split(h @ w_in, 3, axis=-1)
    q, k, v = (a.reshape(B, S, H, Dh) for a in (q, k, v))
    pos = jnp.arange(S)
    scale = Dh ** -0.5

    def block(i):
        t0 = i * T
        t = t0 + jnp.arange(T)
        qb = lax.dynamic_slice_in_dim(q, t0, T, axis=1)
        z = jnp.einsum('bthd,bshd->bhts', qb, k).astype(jnp.float32) * scale
        causal = pos[None, :] < t[:, None]
        log_1mb = jnp.where(causal, jax.nn.log_sigmoid(-z), 0.0)
        between = lax.cumsum(log_1mb, axis=3, reverse=True) - log_1mb
        a = jnp.where(causal, jnp.exp(jax.nn.log_sigmoid(z) + between), 0.0)
        out = jnp.einsum('bhts,bshd->bthd', a.astype(v.dtype), v)
        return out.reshape(B, T, D)

    o = sweep_query_blocks(block, S // T, B, S)
    return o @ w_out


def dense_swiglu(h, w_gate, w_up, w_down):
    return (jax.nn.silu(h @ w_gate) * (h @ w_up)) @ w_down


def moe_swiglu(h, router, w_gate, w_up, w_down):
    B, S, D = h.shape
    hf = h.reshape(B * S, D)
    logits = (hf @ router).astype(jnp.float32)
    top_val, top_idx = lax.top_k(logits, MOE_TOPK)
    w = jax.nn.softmax(top_val, axis=-1)
    gates = jnp.sum(jax.nn.one_hot(top_idx, N_EXPERTS, dtype=jnp.float32) * w[..., None], axis=1)
    y = jnp.zeros_like(hf)
    for e in range(N_EXPERTS):
        y = y + gates[:, e:e + 1].astype(hf.dtype) * dense_swiglu(hf, w_gate[e], w_up[e], w_down[e])
    return y.reshape(B, S, D)


def setup_inputs(seed: int = 0) -> dict:
    keys = iter(jax.random.split(jax.random.key(seed), 128))

    def nrm(shape, scale):
        return jax.random.normal(next(keys), shape, jnp.float32) * scale

    def gain():
        return 1.0 + nrm((D_MODEL,), 0.02)

    d_in = D_MODEL ** -0.5
    f_in = D_FF ** -0.5
    inp = {'x': nrm((BATCH, SEQ, D_MODEL), 1.0)}
    for i in range(DEPTH):
        p = f'l{i}_'
        inp[p + 'attn_norm'] = gain()
        kind = i % N_MIXERS
        if kind == 0:
            inp[p + 'nsa_w_in'] = nrm((D_MODEL, NSA_IN_DIM), d_in)
            for kv in ('k', 'v'):
                inp[p + f'nsa_cmp{kv}_pos'] = nrm((CMP_BLOCK, HEAD_DIM), 0.2)
                inp[p + f'nsa_cmp{kv}_w1'] = nrm((CMP_BLOCK, HEAD_DIM, CMP_HIDDEN), (CMP_BLOCK * HEAD_DIM) ** -0.5)
                inp[p + f'nsa_cmp{kv}_w2'] = nrm((CMP_HIDDEN, HEAD_DIM), CMP_HIDDEN ** -0.5)
            inp[p + 'nsa_w_out'] = nrm((D_MODEL, D_MODEL), d_in)
        elif kind == 1:
            inp[p + 'moba_w_in'] = nrm((D_MODEL, 3 * D_MODEL), d_in)
            inp[p + 'moba_w_out'] = nrm((D_MODEL, D_MODEL), d_in)
        else:
            inp[p + 'sb_w_in'] = nrm((D_MODEL, 3 * D_MODEL), d_in)
            inp[p + 'sb_w_out'] = nrm((D_MODEL, D_MODEL), d_in)
        inp[p + 'ffn_norm'] = gain()
        if i % 2 == 0:
            inp[p + 'ffn_w_gate'] = nrm((D_MODEL, D_FF), d_in)
            inp[p + 'ffn_w_up'] = nrm((D_MODEL, D_FF), d_in)
            inp[p + 'ffn_w_down'] = nrm((D_FF, D_MODEL), f_in)
        else:
            inp[p + 'moe_router'] = nrm((D_MODEL, N_EXPERTS), d_in)
            inp[p + 'moe_w_gate'] = nrm((N_EXPERTS, D_MODEL, D_FF), d_in)
            inp[p + 'moe_w_up'] = nrm((N_EXPERTS, D_MODEL, D_FF), d_in)
            inp[p + 'moe_w_down'] = nrm((N_EXPERTS, D_FF, D_MODEL), f_in)
    inp['final_norm'] = gain()
    return inp


def reference(x,
              l0_attn_norm, l0_nsa_w_in, l0_nsa_cmpk_pos, l0_nsa_cmpk_w1, l0_nsa_cmpk_w2,
              l0_nsa_cmpv_pos, l0_nsa_cmpv_w1, l0_nsa_cmpv_w2, l0_nsa_w_out,
              l0_ffn_norm, l0_ffn_w_gate, l0_ffn_w_up, l0_ffn_w_down,
              l1_attn_norm, l1_moba_w_in, l1_moba_w_out,
              l1_ffn_norm, l1_moe_router, l1_moe_w_gate, l1_moe_w_up, l1_moe_w_down,
              l2_attn_norm, l2_sb_w_in, l2_sb_w_out,
              l2_ffn_norm, l2_ffn_w_gate, l2_ffn_w_up, l2_ffn_w_down,
              l3_attn_norm, l3_nsa_w_in, l3_nsa_cmpk_pos, l3_nsa_cmpk_w1, l3_nsa_cmpk_w2,
              l3_nsa_cmpv_pos, l3_nsa_cmpv_w1, l3_nsa_cmpv_w2, l3_nsa_w_out,
              l3_ffn_norm, l3_moe_router, l3_moe_w_gate, l3_moe_w_up, l3_moe_w_down,
              final_norm):
    attn_norms = (l0_attn_norm, l1_attn_norm, l2_attn_norm, l3_attn_norm)
    mixer_args = (
        (l0_nsa_w_in, l0_nsa_cmpk_pos, l0_nsa_cmpk_w1, l0_nsa_cmpk_w2,
         l0_nsa_cmpv_pos, l0_nsa_cmpv_w1, l0_nsa_cmpv_w2, l0_nsa_w_out),
        (l1_moba_w_in, l1_moba_w_out),
        (l2_sb_w_in, l2_sb_w_out),
        (l3_nsa_w_in, l3_nsa_cmpk_pos, l3_nsa_cmpk_w1, l3_nsa_cmpk_w2,
         l3_nsa_cmpv_pos, l3_nsa_cmpv_w1, l3_nsa_cmpv_w2, l3_nsa_w_out),
    )
    ffn_norms = (l0_ffn_norm, l1_ffn_norm, l2_ffn_norm, l3_ffn_norm)
    ffn_args = (
        (l0_ffn_w_gate, l0_ffn_w_up, l0_ffn_w_down),
        (l1_moe_router, l1_moe_w_gate, l1_moe_w_up, l1_moe_w_down),
        (l2_ffn_w_gate, l2_ffn_w_up, l2_ffn_w_down),
        (l3_moe_router, l3_moe_w_gate, l3_moe_w_up, l3_moe_w_down),
    )
    mixers = (nsa_mixer, moba_mixer, stick_breaking_mixer)
    for i in range(DEPTH):
        x = x + mixers[i % N_MIXERS](rms_norm(x, attn_norms[i]), *mixer_args[i])
        ffn = dense_swiglu if i % 2 == 0 else moe_swiglu
        x = x + ffn(rms_norm(x, ffn_norms[i]), *ffn_args[i])
    return rms_norm(x, final_norm)
```

```python
import functools

import jax
import jax.numpy as jnp
from jax import lax
from jax.experimental import pallas as pl
from jax.experimental.pallas import tpu as pltpu

F32 = jnp.float32
BF16 = jnp.bfloat16

N_HEADS = 16
HEAD_DIM = 64
LOG2_HEAD_DIM = 6
RMS_EPS = 1e-6
LANES = 128
NSA_GROUPS = 4
NSA_HPG = N_HEADS // NSA_GROUPS
NSA_GW = NSA_HPG * HEAD_DIM
CMP_BLOCK = 32
CMP_STRIDE = 16
CMP_HIDDEN = 2 * HEAD_DIM
SEL_BLOCK = 64
LOG2_SEL_BLOCK = 6
SEL_TOPN = 16
WINDOW = 512
MOBA_BLOCK = 256
MOBA_TOPK = 3
N_EXPERTS = 8
NEG = -1e30
PICKED = -3e38
VMEM_LIMIT = 52 * 1024 * 1024

NSA_TQ = 128
NSA_TK = 512
ATT_TQ = 256
PROJ_TM = 512
FFN_TM = 512
FFN_TF = 512
MOE_TM = 512
GATHER_R = 256


def _cparams(*sem):
    return pltpu.CompilerParams(dimension_semantics=sem, vmem_limit_bytes=VMEM_LIMIT)


def _dot(a, b):
    return jnp.dot(a, b, preferred_element_type=F32)


def _dot_nt(a, b):
    return lax.dot_general(a, b, (((1,), (1,)), ((), ())), preferred_element_type=F32)


def _split(a):
    hi = a.astype(BF16)
    lo = (a - hi.astype(F32)).astype(BF16)
    return hi, lo


def _rms(x, g):
    ms = jnp.mean(x * x, axis=-1, keepdims=True)
    return x * lax.rsqrt(ms + RMS_EPS) * g


def _sigmoid(x):
    return 1.0 / (1.0 + jnp.exp(-x))


def _first_argmax(x, lane_f):
    m = jnp.max(x, axis=1, keepdims=True)
    first = jnp.min(jnp.where(x == m, lane_f, 1e9), axis=1, keepdims=True)
    return m, first


def _norm_proj_kernel(x_ref, g_ref, *refs):
    n = len(refs) // 2
    h = _rms(x_ref[...], g_ref[...]).astype(BF16)
    for w_ref, o_ref in zip(refs[:n], refs[n:]):
        o_ref[...] = _dot(h, w_ref[...]).astype(o_ref.dtype)


def norm_proj(x2, g, ws, out_dtypes):
    n_tok, d = x2.shape
    tm = min(PROJ_TM, n_tok)
    in_specs = [pl.BlockSpec((tm, d), lambda i: (i, 0)), pl.BlockSpec((1, d), lambda i: (0, 0))]
    in_specs += [pl.BlockSpec(w.shape, lambda i: (0, 0)) for w in ws]
    out_specs = [pl.BlockSpec((tm, w.shape[1]), lambda i: (i, 0)) for w in ws]
    out_shape = [jax.ShapeDtypeStruct((n_tok, w.shape[1]), dt) for w, dt in zip(ws, out_dtypes)]
    return pl.pallas_call(
        _norm_proj_kernel, grid=(n_tok // tm,), in_specs=in_specs, out_specs=out_specs,
        out_shape=out_shape, compiler_params=_cparams("parallel"), name="norm_proj",
    )(x2, g.reshape(1, d), *ws)


def _mm_res_kernel(a_ref, w_ref, r_ref, o_ref):
    o_ref[...] = r_ref[...] + _dot(a_ref[...], w_ref[...])


def matmul_residual(a, w, res):
    n_tok, k = a.shape
    d = w.shape[1]
    tm = min(PROJ_TM, n_tok)
    return pl.pallas_call(
        _mm_res_kernel, grid=(n_tok // tm,),
        in_specs=[pl.BlockSpec((tm, k), lambda i: (i, 0)), pl.BlockSpec((k, d), lambda i: (0, 0)),
                  pl.BlockSpec((tm, d), lambda i: (i, 0))],
        out_specs=pl.BlockSpec((tm, d), lambda i: (i, 0)),
        out_shape=jax.ShapeDtypeStruct((n_tok, d), F32),
        compiler_params=_cparams("parallel"), name="out_proj",
    )(a, w, res)


def _ffn_kernel(x_ref, g_ref, wg_ref, wu_ref, wd_ref, o_ref, h_ref, acc_ref):
    j = pl.program_id(1)

    @pl.when(j == 0)
    def _():
        h_ref[...] = _rms(x_ref[...], g_ref[...]).astype(BF16)
        acc_ref[...] = jnp.zeros_like(acc_ref)

    h = h_ref[...]
    a = _dot(h, wg_ref[...])
    u = _dot(h, wu_ref[...])
    act = (a * _sigmoid(a) * u).astype(BF16)
    acc_ref[...] += _dot(act, wd_ref[...])

    @pl.when(j == pl.num_programs(1) - 1)
    def _():
        o_ref[...] = x_ref[...] + acc_ref[...]


def dense_ffn(x2, g, wg, wu, wd):
    n_tok, d = x2.shape
    f = wg.shape[1]
    tm = min(FFN_TM, n_tok)
    tf = FFN_TF
    return pl.pallas_call(
        _ffn_kernel, grid=(n_tok // tm, f // tf),
        in_specs=[pl.BlockSpec((tm, d), lambda i, j: (i, 0)), pl.BlockSpec((1, d), lambda i, j: (0, 0)),
                  pl.BlockSpec((d, tf), lambda i, j: (0, j)), pl.BlockSpec((d, tf), lambda i, j: (0, j)),
                  pl.BlockSpec((tf, d), lambda i, j: (j, 0))],
        out_specs=pl.BlockSpec((tm, d), lambda i, j: (i, 0)),
        out_shape=jax.ShapeDtypeStruct((n_tok, d), F32),
        scratch_shapes=[pltpu.VMEM((tm, d), BF16), pltpu.VMEM((tm, d), F32)],
        compiler_params=_cparams("parallel", "arbitrary"), name="dense_ffn",
    )(x2, g.reshape(1, d), wg, wu, wd)


def _compress_kernel(c_ref, w1_ref, pos_ref, w2_ref, o_ref):
    c = c_ref[0, 0]
    n_chunks, cw = c.shape
    w1 = w1_ref[...]
    w1a = w1[:cw].astype(BF16)
    w1b = w1[cw:].astype(BF16)
    ph, plo = _split(pos_ref[...])
    wh, wlo = _split(w1)
    c0 = (_dot(ph, wh) + _dot(ph, wlo) + _dot(plo, wh))[0:1]
    a = _dot(c, w1a)
    b = _dot(c, w1b)
    hid = a + pltpu.roll(b, n_chunks - 1, 0) + c0
    act = 0.5 * hid * (1.0 + jnp.tanh(0.7978845608028654 * (hid + 0.044715 * hid * hid * hid)))
    o_ref[0, 0] = _dot(act.astype(BF16), w2_ref[...]).astype(o_ref.dtype)


def nsa_compress(chunks, pos, w1, w2):
    b, g, n_chunks, cw = chunks.shape
    w1f = w1.reshape(CMP_BLOCK * HEAD_DIM, CMP_HIDDEN)
    pos8 = jnp.broadcast_to(pos.reshape(1, CMP_BLOCK * HEAD_DIM), (8, CMP_BLOCK * HEAD_DIM))
    w2rep = jnp.tile(w2, (1, NSA_HPG)).astype(BF16)
    return pl.pallas_call(
        _compress_kernel, grid=(b, g),
        in_specs=[pl.BlockSpec((1, 1, n_chunks, cw), lambda i, j: (i, j, 0, 0)),
                  pl.BlockSpec(w1f.shape, lambda i, j: (0, 0)),
                  pl.BlockSpec(pos8.shape, lambda i, j: (0, 0)),
                  pl.BlockSpec(w2rep.shape, lambda i, j: (0, 0))],
        out_specs=pl.BlockSpec((1, 1, n_chunks, NSA_GW), lambda i, j: (i, j, 0, 0)),
        out_shape=jax.ShapeDtypeStruct((b, g, n_chunks, NSA_GW), BF16),
        compiler_params=_cparams("parallel", "parallel"), name="nsa_compress",
    )(chunks, w1f, pos8, w2rep)


def _stack_heads(q, n_heads):
    slot = lax.broadcasted_iota(jnp.int32, q.shape, 1) >> LOG2_HEAD_DIM
    return jnp.concatenate([jnp.where(slot == p, q, jnp.zeros_like(q)) for p in range(n_heads)], axis=0)


def _nsa_kernel(slopes_ref, q_ref, gl_ref, kc_ref, vc_ref, ov_ref, ks_ref, vs_ref, kw_ref, vw_ref,
                o_ref, acc_ref, m_ref, l_ref, *, top_n):
    g = pl.program_id(1)
    qi = pl.program_id(2)
    tq = q_ref.shape[1]
    n_cmp = kc_ref.shape[2]
    hp = NSA_HPG
    t0 = qi * tq
    slopes = [slopes_ref[g * hp + p] for p in range(hp)]

    qs = _stack_heads(q_ref[0] * jnp.asarray(HEAD_DIM ** -0.5, BF16), hp)


    s_c = _dot_nt(qs, kc_ref[0, 0])
    n_idx = lax.broadcasted_iota(jnp.int32, (tq, n_cmp), 1)
    t_idx = t0 + lax.broadcasted_iota(jnp.int32, (tq, n_cmp), 0)
    valid_c = (n_idx * CMP_STRIDE + (CMP_BLOCK - 1)) <= t_idx
    cpos = (n_idx * CMP_STRIDE - t0).astype(F32) + 0.5 * (CMP_BLOCK - 1)
    p_sum = jnp.zeros((tq, n_cmp), F32)
    p_list = []
    for p in range(hp):
        sp = jnp.where(valid_c, s_c[p * tq:(p + 1) * tq] + slopes[p] * cpos, NEG)
        m = jnp.max(sp, axis=1, keepdims=True)
        e = jnp.where(valid_c, jnp.exp(sp - m), 0.0)
        pc = e * (1.0 / jnp.maximum(jnp.sum(e, axis=1, keepdims=True), 1e-30))
        p_sum = p_sum + pc
        p_list.append(pc.astype(BF16))
    o_c = _dot(jnp.concatenate(p_list, axis=0), vc_ref[0, 0])

    ps_hi, ps_lo = _split(p_sum)
    imp = _dot(ps_hi, ov_ref[...]) + _dot(ps_lo, ov_ref[...])
    j_idx = lax.broadcasted_iota(jnp.int32, (tq, LANES), 1)
    j_f = j_idx.astype(F32)
    cur = (t0 + lax.broadcasted_iota(jnp.int32, (tq, LANES), 0)) >> LOG2_SEL_BLOCK
    forced = (j_idx == 0) | (j_idx == cur) | (j_idx == cur - 1)
    x0 = jnp.where(j_idx > cur, NEG, jnp.where(forced, -NEG, imp))

    def pick(_, carry):
        x, sel = carry
        _, first = _first_argmax(x, j_f)
        hit = j_f == first
        return jnp.where(hit, PICKED, x), jnp.where(hit, 0.0, sel)

    _, sel_neg = lax.fori_loop(0, top_n, pick, (x0, jnp.full((tq, LANES), NEG, F32)))
    sel_neg = sel_neg.astype(BF16)

    tk = NSA_TK
    m_ref[...] = jnp.full_like(m_ref, NEG)
    l_ref[...] = jnp.zeros_like(l_ref)
    acc_ref[...] = jnp.zeros_like(acc_ref)
    blocks_per_tile = tk // SEL_BLOCK

    def sel_tile(kt, carry):
        k0 = pl.multiple_of(kt * tk, tk)
        s = _dot_nt(qs, ks_ref[0, pl.ds(k0, tk), :])
        e_j = lax.broadcasted_iota(jnp.int32, (LANES, tk), 0)
        e_c = lax.broadcasted_iota(jnp.int32, (LANES, tk), 1)
        expand = jnp.where(e_j == kt * blocks_per_tile + (e_c >> LOG2_SEL_BLOCK), 1.0, 0.0).astype(BF16)
        mask = _dot(sel_neg, expand)
        pos = k0 + lax.broadcasted_iota(jnp.int32, (tq, tk), 1)
        t = t0 + lax.broadcasted_iota(jnp.int32, (tq, tk), 0)
        mask = jnp.where(pos <= t, mask, NEG)
        rel = (pos - t0).astype(F32)
        p_tiles, alphas = [], []
        for p in range(hp):
            rows = slice(p * tq, (p + 1) * tq)
            sp = s[rows] + (mask + slopes[p] * rel)
            m_old = m_ref[rows]
            m_new = jnp.maximum(m_old, jnp.max(sp, axis=1, keepdims=True))
            alpha = jnp.exp(m_old - m_new)
            e = jnp.exp(sp - m_new)
            l_ref[rows] = alpha * l_ref[rows] + jnp.sum(e, axis=1, keepdims=True)
            m_ref[rows] = m_new
            p_tiles.append(e.astype(BF16))
            alphas.append(alpha)
        pv = _dot(jnp.concatenate(p_tiles, axis=0), vs_ref[0, pl.ds(k0, tk), :])
        acc_ref[...] = acc_ref[...] * jnp.concatenate(alphas, axis=0) + pv
        return carry

    lax.fori_loop(0, (t0 + tq + tk - 1) // tk, sel_tile, 0)

    wlen = WINDOW + tq
    w0 = pl.multiple_of(jnp.maximum(t0 - WINDOW, 0), tq)
    s_w = _dot_nt(qs, kw_ref[0, pl.ds(w0, wlen), :])
    pos_w = w0 + lax.broadcasted_iota(jnp.int32, (tq, wlen), 1)
    dist_w = t0 + lax.broadcasted_iota(jnp.int32, (tq, wlen), 0) - pos_w
    valid_w = (dist_w >= 0) & (dist_w < WINDOW)
    rel_w = (pos_w - t0).astype(F32)
    pw_list, lw = [], []
    for p in range(hp):
        sp = jnp.where(valid_w, s_w[p * tq:(p + 1) * tq] + slopes[p] * rel_w, NEG)
        m = jnp.max(sp, axis=1, keepdims=True)
        e = jnp.exp(sp - m)
        lw.append(jnp.sum(e, axis=1, keepdims=True))
        pw_list.append(e.astype(BF16))
    o_w = _dot(jnp.concatenate(pw_list, axis=0), vw_ref[0, pl.ds(w0, wlen), :])

    gates = _sigmoid(gl_ref[0, 0])
    slot = lax.broadcasted_iota(jnp.int32, (tq, NSA_GW), 1) >> LOG2_HEAD_DIM
    out = jnp.zeros((tq, NSA_GW), F32)
    for p in range(hp):
        rows = slice(p * tq, (p + 1) * tq)
        g_c = gates[:, p:p + 1]
        g_s = gates[:, hp + p:hp + p + 1] * (1.0 / l_ref[rows])
        g_w = gates[:, 2 * hp + p:2 * hp + p + 1] * (1.0 / lw[p])
        o_p = g_c * o_c[rows] + g_s * acc_ref[rows] + g_w * o_w[rows]
        out = jnp.where(slot == p, o_p, out)
    o_ref[0] = out.astype(o_ref.dtype)


def nsa_attention(q, gl, k_cmp, v_cmp, ks4, vs4, kw4, vw4, slopes):
    b, s, d = q.shape
    g = NSA_GROUPS
    tq = NSA_TQ
    n_cmp = k_cmp.shape[2]
    n_sel = s // SEL_BLOCK
    n = jnp.arange(n_cmp)[:, None]
    j = jnp.arange(LANES)[None, :]
    overlap = ((n * CMP_STRIDE < (j + 1) * SEL_BLOCK) & (n * CMP_STRIDE + CMP_BLOCK - 1 >= j * SEL_BLOCK)
               & (j < n_sel) & (n < n_cmp - 1)).astype(BF16)
    kv_spec = pl.BlockSpec((1, s, NSA_GW), lambda i, gg, qq: (i, 0, gg))
    cmp_spec = pl.BlockSpec((1, 1, n_cmp, NSA_GW), lambda i, gg, qq: (i, gg, 0, 0))
    return pl.pallas_call(
        functools.partial(_nsa_kernel, top_n=min(SEL_TOPN, n_sel)),
        grid=(b, g, s // tq),
        in_specs=[pl.BlockSpec(memory_space=pltpu.SMEM),
                  pl.BlockSpec((1, tq, NSA_GW), lambda i, gg, qq: (i, qq, gg)),
                  pl.BlockSpec((1, 1, tq, 3 * NSA_HPG), lambda i, gg, qq: (i, gg, qq, 0)),
                  cmp_spec, cmp_spec,
                  pl.BlockSpec(overlap.shape, lambda i, gg, qq: (0, 0)),
                  kv_spec, kv_spec, kv_spec, kv_spec],
        out_specs=pl.BlockSpec((1, tq, NSA_GW), lambda i, gg, qq: (i, qq, gg)),
        out_shape=jax.ShapeDtypeStruct((b, s, d), BF16),
        scratch_shapes=[pltpu.VMEM((NSA_HPG * tq, NSA_GW), F32), pltpu.VMEM((NSA_HPG * tq, 1), F32),
                        pltpu.VMEM((NSA_HPG * tq, 1), F32)],
        compiler_params=_cparams("parallel", "parallel", "arbitrary"), name="nsa_attention",
    )(slopes, q, gl, k_cmp, v_cmp, overlap, ks4, vs4, kw4, vw4)


def _pair_rows_iota(tq, width, dim):
    return lax.broadcasted_iota(jnp.int32, (2 * tq, width), dim)


def _moba_kernel(slopes_ref, q_ref, k_ref, v_ref, o_ref, kmean_ref, acc_ref, m_ref, l_ref):
    hpair = pl.program_id(1)
    qi = pl.program_id(2)
    tq = q_ref.shape[1]
    blk = MOBA_BLOCK
    n_blk = k_ref.shape[1] // blk
    t0 = qi * tq

    @pl.when(qi == 0)
    def _():
        kmean_ref[...] = jnp.zeros_like(kmean_ref)
        for n in range(n_blk):
            kmean_ref[n:n + 1, :] = jnp.mean(k_ref[0, n * blk:(n + 1) * blk, :].astype(F32), axis=0, keepdims=True)

    qs = _stack_heads(q_ref[0] * jnp.asarray(HEAD_DIM ** -0.5, BF16), 2)
    row = _pair_rows_iota(tq, blk, 0)
    slope = jnp.where(row < tq, slopes_ref[2 * hpair], slopes_ref[2 * hpair + 1])[:, 0:1]

    km_hi, km_lo = _split(kmean_ref[...])
    gate = _dot_nt(qs, km_hi) + _dot_nt(qs, km_lo)
    n_idx = _pair_rows_iota(tq, LANES, 1)
    n_f = n_idx.astype(F32)
    past = n_idx < qi
    x0 = jnp.where(past, gate, NEG)

    def pick(_, carry):
        x, sel = carry
        _, first = _first_argmax(x, n_f)
        hit = n_f == first
        return jnp.where(hit, PICKED, x), jnp.where(hit, 0.0, sel)

    _, sel_neg = lax.fori_loop(0, min(MOBA_TOPK, n_blk), pick, (x0, jnp.full((2 * tq, LANES), NEG, F32)))
    sel_neg = jnp.where(past, sel_neg, NEG).astype(BF16)

    t_row = t0 + (row & (tq - 1))
    col = _pair_rows_iota(tq, blk, 1)

    k0 = pl.multiple_of(qi * blk, blk)
    pos = k0 + col
    s = _dot_nt(qs, k_ref[0, pl.ds(k0, blk), :]) + slope * (pos - t0).astype(F32)
    s = jnp.where(pos <= t_row, s, NEG)
    m0 = jnp.max(s, axis=1, keepdims=True)
    e = jnp.exp(s - m0)
    m_ref[...] = m0
    l_ref[...] = jnp.sum(e, axis=1, keepdims=True)
    acc_ref[...] = _dot(e.astype(BF16), v_ref[0, pl.ds(k0, blk), :])

    def past_block(n, carry):
        kn = pl.multiple_of(n * blk, blk)
        e_j = lax.broadcasted_iota(jnp.int32, (LANES, blk), 0)
        expand = jnp.where(e_j == n, 1.0, 0.0).astype(BF16)
        mask = _dot(sel_neg, expand)
        sp = _dot_nt(qs, k_ref[0, pl.ds(kn, blk), :]) + (mask + slope * (kn + col - t0).astype(F32))
        m_old = m_ref[...]
        m_new = jnp.maximum(m_old, jnp.max(sp, axis=1, keepdims=True))
        alpha = jnp.exp(m_old - m_new)
        ee = jnp.exp(sp - m_new)
        l_ref[...] = alpha * l_ref[...] + jnp.sum(ee, axis=1, keepdims=True)
        m_ref[...] = m_new
        acc_ref[...] = alpha * acc_ref[...] + _dot(ee.astype(BF16), v_ref[0, pl.ds(kn, blk), :])
        return carry

    lax.fori_loop(0, qi, past_block, 0)

    o = acc_ref[...] * (1.0 / l_ref[...])
    lane = lax.broadcasted_iota(jnp.int32, (tq, 2 * HEAD_DIM), 1)
    o_ref[0] = jnp.where(lane < HEAD_DIM, o[:tq], o[tq:]).astype(o_ref.dtype)


def moba_attention(q, k, v, slopes):
    b, s, d = q.shape
    tq = ATT_TQ
    pw = 2 * HEAD_DIM
    assert tq == MOBA_BLOCK and s % MOBA_BLOCK == 0
    kv_spec = pl.BlockSpec((1, s, pw), lambda i, h, qq: (i, 0, h))
    q_spec = pl.BlockSpec((1, tq, pw), lambda i, h, qq: (i, qq, h))
    return pl.pallas_call(
        _moba_kernel, grid=(b, N_HEADS // 2, s // tq),
        in_specs=[pl.BlockSpec(memory_space=pltpu.SMEM), q_spec, kv_spec, kv_spec],
        out_specs=q_spec,
        out_shape=jax.ShapeDtypeStruct((b, s, d), BF16),
        scratch_shapes=[pltpu.VMEM((LANES, pw), F32), pltpu.VMEM((2 * tq, pw), F32),
                        pltpu.VMEM((2 * tq, 1), F32), pltpu.VMEM((2 * tq, 1), F32)],
        compiler_params=_cparams("parallel", "parallel", "arbitrary"), name="moba_attention",
    )(slopes, q, k, v)


def _sb_kernel(q_ref, k_ref, v_ref, u_ref, o_ref, acc_ref):
    qi = pl.program_id(2)
    tq = q_ref.shape[1]
    tk = u_ref.shape[0]
    t0 = qi * tq
    qs = _stack_heads(q_ref[0] * jnp.asarray(HEAD_DIM ** -0.5, BF16), 2)
    t_row = t0 + (_pair_rows_iota(tq, tk, 0) & (tq - 1))
    col = _pair_rows_iota(tq, tk, 1)
    acc_ref[...] = jnp.zeros_like(acc_ref)

    def tile(it, carry):
        j = qi - it
        k0 = pl.multiple_of(j * tk, tk)
        z = _dot_nt(qs, k_ref[0, pl.ds(k0, tk), :])
        causal = (k0 + col) < t_row
        lg = jnp.where(causal, -(jnp.maximum(z, 0.0) + jnp.log(1.0 + jnp.exp(-jnp.abs(z)))), 0.0)
        l_hi, l_lo = _split(lg)
        c = _dot(l_hi, u_ref[...]) + _dot(l_lo, u_ref[...]) + carry
        a = jnp.where(causal, jnp.exp(z + c), 0.0)
        acc_ref[...] += _dot(a.astype(BF16), v_ref[0, pl.ds(k0, tk), :])
        return c[:, 0:1]

    lax.fori_loop(0, qi + 1, tile, jnp.zeros((2 * tq, 1), F32))
    o = acc_ref[...]
    lane = lax.broadcasted_iota(jnp.int32, (tq, 2 * HEAD_DIM), 1)
    o_ref[0] = jnp.where(lane < HEAD_DIM, o[:tq], o[tq:]).astype(o_ref.dtype)


def sb_attention(q, k, v):
    b, s, d = q.shape
    tq = ATT_TQ
    pw = 2 * HEAD_DIM
    r = jnp.arange(tq)
    u = (r[:, None] >= r[None, :]).astype(BF16)
    kv_spec = pl.BlockSpec((1, s, pw), lambda i, h, qq: (i, 0, h))
    q_spec = pl.BlockSpec((1, tq, pw), lambda i, h, qq: (i, qq, h))
    return pl.pallas_call(
        _sb_kernel, grid=(b, N_HEADS // 2, s // tq),
        in_specs=[q_spec, kv_spec, kv_spec, pl.BlockSpec((tq, tq), lambda i, h, qq: (0, 0))],
        out_specs=q_spec,
        out_shape=jax.ShapeDtypeStruct((b, s, d), BF16),
        scratch_shapes=[pltpu.VMEM((2 * tq, pw), F32)],
        compiler_params=_cparams("parallel", "parallel", "arbitrary"), name="sb_attention",
    )(q, k, v, u)


def _router_kernel(x_ref, g_ref, r_ref, o_ref):
    h = _rms(x_ref[...], g_ref[...])
    h_hi, h_lo = _split(h)
    r_hi, r_lo = _split(r_ref[...])
    logits = _dot(h_hi, r_hi) + _dot(h_hi, r_lo) + _dot(h_lo, r_hi)
    lane = lax.broadcasted_iota(jnp.int32, logits.shape, 1)
    lane_f = lane.astype(F32)
    x = jnp.where(lane < N_EXPERTS, logits, NEG)
    m1, i1 = _first_argmax(x, lane_f)
    m2, i2 = _first_argmax(jnp.where(lane_f == i1, PICKED, x), lane_f)
    e = jnp.exp(m2 - m1)
    w1 = 1.0 / (1.0 + e)
    w2 = e * w1
    o_ref[...] = jnp.where(lane == 0, i1, jnp.where(lane == 1, i2, jnp.where(lane == 2, w1, jnp.where(lane == 3, w2, 0.0))))


def moe_router(x2, g, router):
    n_tok, d = x2.shape
    tm = min(PROJ_TM, n_tok)
    r_pad = jnp.pad(router, ((0, 0), (0, LANES - router.shape[1])))
    return pl.pallas_call(
        _router_kernel, grid=(n_tok // tm,),
        in_specs=[pl.BlockSpec((tm, d), lambda i: (i, 0)), pl.BlockSpec((1, d), lambda i: (0, 0)),
                  pl.BlockSpec((d, LANES), lambda i: (0, 0))],
        out_specs=pl.BlockSpec((tm, LANES), lambda i: (i, 0)),
        out_shape=jax.ShapeDtypeStruct((n_tok, LANES), F32),
        compiler_params=_cparams("parallel"), name="moe_router",
    )(x2, g.reshape(1, d), r_pad)


def _row_copy(src_ref, dst_ref, src_row, dst_row, sem):
    return pltpu.make_async_copy(src_ref.at[pl.ds(src_row, 1)], dst_ref.at[pl.ds(dst_row, 1)], sem)


def _gather_kernel(idx_ref, src_ref, o_ref, sem):
    rows = o_ref.shape[0]

    def start(r, c):
        _row_copy(src_ref, o_ref, idx_ref[0, 0, r], r, sem).start()
        return c

    def wait(r, c):
        _row_copy(src_ref, o_ref, 0, r, sem).wait()
        return c

    lax.fori_loop(0, rows, start, 0)
    lax.fori_loop(0, rows, wait, 0)


def gather_rows(src, idx):
    n = idx.shape[0]
    d = src.shape[1]
    r = GATHER_R
    return pl.pallas_call(
        _gather_kernel, grid=(n // r,),
        in_specs=[pl.BlockSpec((1, 1, r), lambda i: (i, 0, 0), memory_space=pltpu.SMEM),
                  pl.BlockSpec(memory_space=pl.ANY)],
        out_specs=pl.BlockSpec((r, d), lambda i: (i, 0)),
        out_shape=jax.ShapeDtypeStruct((n, d), src.dtype),
        scratch_shapes=[pltpu.SemaphoreType.DMA(())],
        compiler_params=_cparams("arbitrary"), name="moe_dispatch",
    )(idx.reshape(n // r, 1, r), src)


def _moe_ffn_kernel(te_ref, nu_ref, x_ref, ws_ref, g_ref, wg_ref, wu_ref, wd_ref, o_ref, h_ref, acc_ref):
    i = pl.program_id(0)
    j = pl.program_id(1)
    used = i < nu_ref[0]

    @pl.when(j == 0)
    def _():
        h_ref[...] = _rms(x_ref[...], g_ref[...]).astype(BF16)
        acc_ref[...] = jnp.zeros_like(acc_ref)

    @pl.when(used)
    def _():
        h = h_ref[...]
        a = _dot(h, wg_ref[0])
        u = _dot(h, wu_ref[0])
        act = (a * _sigmoid(a) * u).astype(BF16)
        acc_ref[...] += _dot(act, wd_ref[0])

    @pl.when(j == pl.num_programs(1) - 1)
    def _():
        o_ref[...] = acc_ref[...] * ws_ref[...]


def moe_ffn(x_sorted, w_slot, g, wg, wu, wd, tile_expert, n_used):
    n_slots, d = x_sorted.shape
    f = wg.shape[2]
    tm, tf = MOE_TM, FFN_TF
    nf = f // tf

    def fidx(i, j, nu):
        return jnp.where(i < nu[0], j, nf - 1)

    grid_spec = pltpu.PrefetchScalarGridSpec(
        num_scalar_prefetch=2, grid=(n_slots // tm, nf),
        in_specs=[pl.BlockSpec((tm, d), lambda i, j, te, nu: (i, 0)),
                  pl.BlockSpec((tm, 1), lambda i, j, te, nu: (i, 0)),
                  pl.BlockSpec((1, d), lambda i, j, te, nu: (0, 0)),
                  pl.BlockSpec((1, d, tf), lambda i, j, te, nu: (te[i], 0, fidx(i, j, nu))),
                  pl.BlockSpec((1, d, tf), lambda i, j, te, nu: (te[i], 0, fidx(i, j, nu))),
                  pl.BlockSpec((1, tf, d), lambda i, j, te, nu: (te[i], fidx(i, j, nu), 0))],
        out_specs=pl.BlockSpec((tm, d), lambda i, j, te, nu: (i, 0)),
        scratch_shapes=[pltpu.VMEM((tm, d), BF16), pltpu.VMEM((tm, d), F32)])
    return pl.pallas_call(
        _moe_ffn_kernel, grid_spec=grid_spec,
        out_shape=jax.ShapeDtypeStruct((n_slots, d), F32),
        compiler_params=_cparams("arbitrary", "arbitrary"), name="moe_ffn",
    )(tile_expert, n_used, x_sorted, w_slot, g.reshape(1, d), wg, wu, wd)


def _combine_kernel(s1_ref, s2_ref, x_ref, y_ref, g_ref, o_ref, b1_ref, b2_ref, sem, *, final_norm):
    rows = o_ref.shape[0]

    def start(r, c):
        _row_copy(y_ref, b1_ref, s1_ref[0, 0, r], r, sem).start()
        _row_copy(y_ref, b2_ref, s2_ref[0, 0, r], r, sem).start()
        return c

    def wait(r, c):
        _row_copy(y_ref, b1_ref, 0, r, sem).wait()
        _row_copy(y_ref, b2_ref, 0, r, sem).wait()
        return c

    lax.fori_loop(0, rows, start, 0)
    lax.fori_loop(0, rows, wait, 0)
    out = x_ref[...] + b1_ref[...] + b2_ref[...]
    if final_norm:
        out = _rms(out, g_ref[...])
    o_ref[...] = out


def moe_combine(x2, y, slot1, slot2, final_g):
    n_tok, d = x2.shape
    r = min(GATHER_R, n_tok)
    final_norm = final_g is not None
    g = final_g if final_norm else jnp.ones((d,), F32)
    idx_spec = pl.BlockSpec((1, 1, r), lambda i: (i, 0, 0), memory_space=pltpu.SMEM)
    return pl.pallas_call(
        functools.partial(_combine_kernel, final_norm=final_norm), grid=(n_tok // r,),
        in_specs=[idx_spec, idx_spec, pl.BlockSpec((r, d), lambda i: (i, 0)),
                  pl.BlockSpec(memory_space=pl.ANY), pl.BlockSpec((1, d), lambda i: (0, 0))],
        out_specs=pl.BlockSpec((r, d), lambda i: (i, 0)),
        out_shape=jax.ShapeDtypeStruct((n_tok, d), F32),
        scratch_shapes=[pltpu.VMEM((r, d), F32), pltpu.VMEM((r, d), F32), pltpu.SemaphoreType.DMA(())],
        compiler_params=_cparams("arbitrary"), name="moe_combine",
    )(slot1.reshape(n_tok // r, 1, r), slot2.reshape(n_tok // r, 1, r), x2, y, g.reshape(1, d))


def moe_layer(x2, g, router, wg, wu, wd, final_g=None):
    n_tok, d = x2.shape
    tm = MOE_TM
    n_e = router.shape[1]
    info = moe_router(x2, g, router)
    expert = info[:, :2].astype(jnp.int32).reshape(-1)
    weight = info[:, 2:4].reshape(-1)
    onehot = (expert[:, None] == jnp.arange(n_e)[None, :]).astype(jnp.int32)
    csum = jnp.cumsum(onehot, axis=0)
    rank = jnp.take_along_axis(csum, expert[:, None], axis=1)[:, 0] - 1
    padded = ((csum[-1] + tm - 1) // tm) * tm
    ends = jnp.cumsum(padded)
    slot = (ends - padded)[expert] + rank
    n_slots = 2 * n_tok + n_e * tm
    n_tiles = n_slots // tm
    token_of_slot = jnp.zeros((n_slots,), jnp.int32).at[slot].set(jnp.arange(2 * n_tok, dtype=jnp.int32) // 2)
    weight_of_slot = jnp.zeros((n_slots,), F32).at[slot].set(weight)
    n_used = (ends[-1] // tm).astype(jnp.int32)
    tile_start = jnp.minimum(jnp.arange(n_tiles, dtype=jnp.int32), n_used - 1) * tm
    tile_expert = jnp.minimum(jnp.searchsorted(ends, tile_start, side="right"), n_e - 1).astype(jnp.int32)

    x_sorted = gather_rows(x2, token_of_slot)
    y = moe_ffn(x_sorted, weight_of_slot.reshape(n_slots, 1), g, wg, wu, wd, tile_expert, n_used.reshape(1))
    slot2 = slot.reshape(n_tok, 2)
    return moe_combine(x2, y, slot2[:, 0], slot2[:, 1], final_g)


def _alibi_slopes():
    return 2.0 ** (-8.0 * jnp.arange(1, N_HEADS + 1, dtype=F32) / N_HEADS)


def nsa_layer(x, g, w_in, cmpk, cmpv, w_out):
    b, s, d = x.shape
    n_tok = b * s
    x2 = x.reshape(n_tok, d)
    grp, hpg, dh = NSA_GROUPS, NSA_HPG, HEAD_DIM
    kvd = grp * dh
    w_q = w_in[:, :d].astype(BF16)
    w_kv = w_in[:, d:d + 6 * kvd].astype(BF16)
    w_g = jnp.pad(w_in[:, d + 6 * kvd:], ((0, 0), (0, LANES - 3 * N_HEADS))).astype(BF16)
    q, kv, gl = norm_proj(x2, g, [w_q, w_kv, w_g], [BF16, BF16, F32])
    kc, vc, ks, vs, kw, vw = (kv[:, i * kvd:(i + 1) * kvd] for i in range(6))

    def chunks(a):
        a = a.reshape(b, s // CMP_STRIDE, CMP_STRIDE, grp, dh).transpose(0, 3, 1, 2, 4)
        return a.reshape(b, grp, s // CMP_STRIDE, CMP_STRIDE * dh)

    def rep(a):
        a = jnp.broadcast_to(a.reshape(b, s, grp, 1, dh), (b, s, grp, hpg, dh))
        return a.reshape(b, s, grp * hpg * dh)

    k_cmp = nsa_compress(chunks(kc), *cmpk)
    v_cmp = nsa_compress(chunks(vc), *cmpv)
    gl = gl[:, :3 * N_HEADS].reshape(b, s, 3, grp, hpg).transpose(0, 3, 1, 2, 4).reshape(b, grp, s, 3 * hpg)
    o = nsa_attention(q.reshape(b, s, d), gl, k_cmp, v_cmp, rep(ks), rep(vs), rep(kw), rep(vw), _alibi_slopes())
    return matmul_residual(o.reshape(n_tok, d), w_out.astype(BF16), x2).reshape(b, s, d)


def _qkv_layer(x, g, w_in, w_out, attend):
    b, s, d = x.shape
    n_tok = b * s
    x2 = x.reshape(n_tok, d)
    ws = [w_in[:, i * d:(i + 1) * d].astype(BF16) for i in range(3)]
    q, k, v = (a.reshape(b, s, d) for a in norm_proj(x2, g, ws, [BF16] * 3))
    o = attend(q, k, v)
    return matmul_residual(o.reshape(n_tok, d), w_out.astype(BF16), x2).reshape(b, s, d)


def moba_layer(x, g, w_in, w_out):
    slopes = _alibi_slopes()
    return _qkv_layer(x, g, w_in, w_out, lambda q, k, v: moba_attention(q, k, v, slopes))


def sb_layer(x, g, w_in, w_out):
    return _qkv_layer(x, g, w_in, w_out, sb_attention)


def dense_ffn_layer(x, g, wg, wu, wd):
    b, s, d = x.shape
    return dense_ffn(x.reshape(b * s, d), g, wg.astype(BF16), wu.astype(BF16), wd.astype(BF16)).reshape(b, s, d)


def moe_ffn_layer(x, g, router, wg, wu, wd, final_g=None):
    b, s, d = x.shape
    return moe_layer(x.reshape(b * s, d), g, router, wg.astype(BF16), wu.astype(BF16), wd.astype(BF16),
                     final_g).reshape(b, s, d)


def kernel(x, l0_attn_norm, l0_nsa_w_in, l0_nsa_cmpk_pos, l0_nsa_cmpk_w1, l0_nsa_cmpk_w2, l0_nsa_cmpv_pos, l0_nsa_cmpv_w1, l0_nsa_cmpv_w2, l0_nsa_w_out, l0_ffn_norm, l0_ffn_w_gate, l0_ffn_w_up, l0_ffn_w_down, l1_attn_norm, l1_moba_w_in, l1_moba_w_out, l1_ffn_norm, l1_moe_router, l1_moe_w_gate, l1_moe_w_up, l1_moe_w_down, l2_attn_norm, l2_sb_w_in, l2_sb_w_out, l2_ffn_norm, l2_ffn_w_gate, l2_ffn_w_up, l2_ffn_w_down, l3_attn_norm, l3_nsa_w_in, l3_nsa_cmpk_pos, l3_nsa_cmpk_w1, l3_nsa_cmpk_w2, l3_nsa_cmpv_pos, l3_nsa_cmpv_w1, l3_nsa_cmpv_w2, l3_nsa_w_out, l3_ffn_norm, l3_moe_router, l3_moe_w_gate, l3_moe_w_up, l3_moe_w_down, final_norm):
    x = nsa_layer(x, l0_attn_norm, l0_nsa_w_in, (l0_nsa_cmpk_pos, l0_nsa_cmpk_w1, l0_nsa_cmpk_w2),
                  (l0_nsa_cmpv_pos, l0_nsa_cmpv_w1, l0_nsa_cmpv_w2), l0_nsa_w_out)
    x = dense_ffn_layer(x, l0_ffn_norm, l0_ffn_w_gate, l0_ffn_w_up, l0_ffn_w_down)
    x = moba_layer(x, l1_attn_norm, l1_moba_w_in, l1_moba_w_out)
    x = moe_ffn_layer(x, l1_ffn_norm, l1_moe_router, l1_moe_w_gate, l1_moe_w_up, l1_moe_w_down)
    x = sb_layer(x, l2_attn_norm, l2_sb_w_in, l2_sb_w_out)
    x = dense_ffn_layer(x, l2_ffn_norm, l2_ffn_w_gate, l2_ffn_w_up, l2_ffn_w_down)
    x = nsa_layer(x, l3_attn_norm, l3_nsa_w_in, (l3_nsa_cmpk_pos, l3_nsa_cmpk_w1, l3_nsa_cmpk_w2),
                  (l3_nsa_cmpv_pos, l3_nsa_cmpv_w1, l3_nsa_cmpv_w2), l3_nsa_w_out)
    return moe_ffn_layer(x, l3_ffn_norm, l3_moe_router, l3_moe_w_gate, l3_moe_w_up, l3_moe_w_down, final_norm)
```

```python
import functools

import jax
import jax.numpy as jnp
from jax import lax
from jax.experimental import pallas as pl
from jax.experimental.pallas import tpu as pltpu

F32 = jnp.float32
BF16 = jnp.bfloat16

N_HEADS = 16
HEAD_DIM = 64
LOG2_HEAD_DIM = 6
RMS_EPS = 1e-6
LANES = 128
NSA_GROUPS = 4
NSA_HPG = N_HEADS // NSA_GROUPS
NSA_GW = NSA_HPG * HEAD_DIM
CMP_BLOCK = 32
CMP_STRIDE = 16
CMP_HIDDEN = 2 * HEAD_DIM
SEL_BLOCK = 64
LOG2_SEL_BLOCK = 6
SEL_TOPN = 16
WINDOW = 512
MOBA_BLOCK = 256
MOBA_TOPK = 3
N_EXPERTS = 8
NEG = -1e30
PICKED = -3e38
VMEM_LIMIT = 52 * 1024 * 1024

ATT_T = 256
N_GATE_ROWS = 16
PROJ_TM = 512
FFN_TM = 512
FFN_TF = 512
MOE_TM = 512
GATHER_R = 256


def _cparams(*sem):
    return pltpu.CompilerParams(dimension_semantics=sem, vmem_limit_bytes=VMEM_LIMIT)


def _dot(a, b):
    return jnp.dot(a, b, preferred_element_type=F32)


def _dot_nt(a, b):
    return lax.dot_general(a, b, (((1,), (1,)), ((), ())), preferred_element_type=F32)


def _split(a):
    hi = a.astype(BF16)
    lo = (a - hi.astype(F32)).astype(BF16)
    return hi, lo


def _rms(x, g):
    ms = jnp.mean(x * x, axis=-1, keepdims=True)
    return x * lax.rsqrt(ms + RMS_EPS) * g


def _sigmoid(x):
    return 1.0 / (1.0 + jnp.exp(-x))


def _first_argmax(x, idx_f, axis):
    m = jnp.max(x, axis=axis, keepdims=True)
    first = jnp.min(jnp.where(x == m, idx_f, 1e9), axis=axis, keepdims=True)
    return m, first


def _iota(shape, dim):
    return lax.broadcasted_iota(jnp.int32, shape, dim)


def _norm_proj_kernel(x_ref, g_ref, *refs):
    n = len(refs) // 2
    h = _rms(x_ref[...], g_ref[...]).astype(BF16)
    for w_ref, o_ref in zip(refs[:n], refs[n:]):
        o_ref[...] = _dot(h, w_ref[...]).astype(o_ref.dtype)


def norm_proj(x2, g, ws, out_dtypes):
    n_tok, d = x2.shape
    tm = min(PROJ_TM, n_tok)
    in_specs = [pl.BlockSpec((tm, d), lambda i: (i, 0)), pl.BlockSpec((1, d), lambda i: (0, 0))]
    in_specs += [pl.BlockSpec(w.shape, lambda i: (0, 0)) for w in ws]
    out_specs = [pl.BlockSpec((tm, w.shape[1]), lambda i: (i, 0)) for w in ws]
    out_shape = [jax.ShapeDtypeStruct((n_tok, w.shape[1]), dt) for w, dt in zip(ws, out_dtypes)]
    return pl.pallas_call(
        _norm_proj_kernel, grid=(n_tok // tm,), in_specs=in_specs, out_specs=out_specs,
        out_shape=out_shape, compiler_params=_cparams("parallel"), name="norm_proj",
    )(x2, g.reshape(1, d), *ws)


def _mm_res_kernel(a_ref, w_ref, r_ref, o_ref):
    o_ref[...] = r_ref[...] + _dot(a_ref[...], w_ref[...])


def matmul_residual(a, w, res):
    n_tok, k = a.shape
    d = w.shape[1]
    tm = min(PROJ_TM, n_tok)
    return pl.pallas_call(
        _mm_res_kernel, grid=(n_tok // tm,),
        in_specs=[pl.BlockSpec((tm, k), lambda i: (i, 0)), pl.BlockSpec((k, d), lambda i: (0, 0)),
                  pl.BlockSpec((tm, d), lambda i: (i, 0))],
        out_specs=pl.BlockSpec((tm, d), lambda i: (i, 0)),
        out_shape=jax.ShapeDtypeStruct((n_tok, d), F32),
        compiler_params=_cparams("parallel"), name="out_proj",
    )(a, w, res)


def _ffn_kernel(x_ref, g_ref, wg_ref, wu_ref, wd_ref, o_ref, h_ref, acc_ref):
    j = pl.program_id(1)

    @pl.when(j == 0)
    def _():
        h_ref[...] = _rms(x_ref[...], g_ref[...]).astype(BF16)
        acc_ref[...] = jnp.zeros_like(acc_ref)

    h = h_ref[...]
    a = _dot(h, wg_ref[...])
    u = _dot(h, wu_ref[...])
    act = (a * _sigmoid(a) * u).astype(BF16)
    acc_ref[...] += _dot(act, wd_ref[...])

    @pl.when(j == pl.num_programs(1) - 1)
    def _():
        o_ref[...] = x_ref[...] + acc_ref[...]


def dense_ffn(x2, g, wg, wu, wd):
    n_tok, d = x2.shape
    f = wg.shape[1]
    tm = min(FFN_TM, n_tok)
    tf = FFN_TF
    return pl.pallas_call(
        _ffn_kernel, grid=(n_tok // tm, f // tf),
        in_specs=[pl.BlockSpec((tm, d), lambda i, j: (i, 0)), pl.BlockSpec((1, d), lambda i, j: (0, 0)),
                  pl.BlockSpec((d, tf), lambda i, j: (0, j)), pl.BlockSpec((d, tf), lambda i, j: (0, j)),
                  pl.BlockSpec((tf, d), lambda i, j: (j, 0))],
        out_specs=pl.BlockSpec((tm, d), lambda i, j: (i, 0)),
        out_shape=jax.ShapeDtypeStruct((n_tok, d), F32),
        scratch_shapes=[pltpu.VMEM((tm, d), BF16), pltpu.VMEM((tm, d), F32)],
        compiler_params=_cparams("parallel", "arbitrary"), name="dense_ffn",
    )(x2, g.reshape(1, d), wg, wu, wd)


def _compress_kernel(c_ref, w1_ref, pos_ref, w2_ref, o_ref, *, transposed):
    c = c_ref[0, 0]
    n_chunks, cw = c.shape
    w1 = w1_ref[...]
    w1a = w1[:cw].astype(BF16)
    w1b = w1[cw:].astype(BF16)
    ph, plo = _split(pos_ref[...])
    wh, wlo = _split(w1)
    c0 = (_dot(ph, wh) + _dot(ph, wlo) + _dot(plo, wh))[0:1]
    a = _dot(c, w1a)
    b = _dot(c, w1b)
    hid = a + pltpu.roll(b, n_chunks - 1, 0) + c0
    act = 0.5 * hid * (1.0 + jnp.tanh(0.7978845608028654 * (hid + 0.044715 * hid * hid * hid)))
    act = act.astype(BF16)
    if transposed:
        o_ref[0, 0] = _dot_nt(w2_ref[...], act).astype(o_ref.dtype)
    else:
        o_ref[0, 0] = _dot(act, w2_ref[...]).astype(o_ref.dtype)


def nsa_compress(chunks, pos, w1, w2, transposed):
    b, g, n_chunks, cw = chunks.shape
    w1f = w1.reshape(CMP_BLOCK * HEAD_DIM, CMP_HIDDEN)
    pos8 = jnp.broadcast_to(pos.reshape(1, CMP_BLOCK * HEAD_DIM), (8, CMP_BLOCK * HEAD_DIM))
    if transposed:
        w2m = w2.T.astype(BF16)
        out_block = (1, 1, HEAD_DIM, n_chunks)
    else:
        w2m = jnp.tile(w2, (1, NSA_HPG)).astype(BF16)
        out_block = (1, 1, n_chunks, NSA_GW)
    return pl.pallas_call(
        functools.partial(_compress_kernel, transposed=transposed), grid=(b, g),
        in_specs=[pl.BlockSpec((1, 1, n_chunks, cw), lambda i, j: (i, j, 0, 0)),
                  pl.BlockSpec(w1f.shape, lambda i, j: (0, 0)),
                  pl.BlockSpec(pos8.shape, lambda i, j: (0, 0)),
                  pl.BlockSpec(w2m.shape, lambda i, j: (0, 0))],
        out_specs=pl.BlockSpec(out_block, lambda i, j: (i, j, 0, 0)),
        out_shape=jax.ShapeDtypeStruct((b, g) + out_block[2:], BF16),
        compiler_params=_cparams("parallel", "parallel"), name="nsa_compress",
    )(chunks, w1f, pos8, w2m)


def _col_max(s):
    k, t = s.shape
    return jnp.max(jnp.max(s.reshape(k // 32, 32, t), axis=0), axis=0, keepdims=True)


def _softmax_step(s, m_old):
    m_new = jnp.maximum(m_old, _col_max(s))
    return m_new, jnp.exp(m_old - m_new), jnp.exp(s - m_new).astype(BF16)


def _pick_top(x0, n_pick):
    idx_f = _iota(x0.shape, 0).astype(F32)

    def pick(_, carry):
        x, sel = carry
        _, first = _first_argmax(x, idx_f, 0)
        hit = idx_f == first
        return jnp.where(hit, PICKED, x), jnp.where(hit, 0.0, sel)

    return lax.fori_loop(0, n_pick, pick, (x0, jnp.full(x0.shape, NEG, F32)))[1]


def _nsa_kernel(slopes_ref, q_ref, glt_ref, kc_ref, vct_ref, ovt_ref, ks_ref, vst_ref, kw_ref, vwt_ref,
                o_ref, sel_ref, acc_ref, p_ref, s_ref, out_ref, *, top_n):
    g = pl.program_id(1)
    qi = pl.program_id(2)
    t = q_ref.shape[1]
    n_cmp = kc_ref.shape[2]
    hp = NSA_HPG
    dh = HEAD_DIM
    t0 = qi * t
    slopes = [slopes_ref[g * hp + p] for p in range(hp)]

    q = q_ref[0] * jnp.asarray(dh ** -0.5, BF16)
    slot = _iota(q.shape, 1) >> LOG2_HEAD_DIM
    qh = [jnp.where(slot == p, q, jnp.zeros_like(q)) for p in range(hp)]
    gates = _sigmoid(glt_ref[0, 0])

    row = _iota((t, t), 0)
    col = _iota((t, t), 1)
    row_f = row.astype(F32)
    ones_rows = jnp.ones((dh, t), BF16)


    n_row = _iota((n_cmp, t), 0)
    valid_c = (n_row * CMP_STRIDE + (CMP_BLOCK - 1)) <= (t0 + _iota((n_cmp, t), 1))
    cpos = (n_row * CMP_STRIDE - t0).astype(F32) + 0.5 * (CMP_BLOCK - 1)
    kc = kc_ref[0, 0]
    vct = vct_ref[0, 0]
    p_sum = jnp.zeros((n_cmp, t), F32)
    for p in range(hp):
        sp = jnp.where(valid_c, _dot_nt(kc, qh[p]) + slopes[p] * cpos, NEG)
        m = jnp.max(sp, axis=0, keepdims=True)
        e = jnp.where(valid_c, jnp.exp(sp - m), 0.0)
        pc = e * (1.0 / jnp.maximum(jnp.sum(e, axis=0, keepdims=True), 1e-30))
        p_sum = p_sum + pc
        out_ref[p * dh:(p + 1) * dh, :] = gates[p:p + 1, :] * _dot(vct, pc.astype(BF16))

    ps_hi, ps_lo = _split(p_sum)
    imp = _dot(ovt_ref[...], ps_hi) + _dot(ovt_ref[...], ps_lo)
    j_idx = _iota(imp.shape, 0)
    cur = (t0 + _iota(imp.shape, 1)) >> LOG2_SEL_BLOCK
    forced = (j_idx == 0) | (j_idx == cur) | (j_idx == cur - 1)
    x0 = jnp.where(j_idx > cur, NEG, jnp.where(forced, -NEG, imp))
    sel_ref[...] = _pick_top(x0, top_n)

    alibi = [slopes[p] * row_f for p in range(hp)]
    causal_add = jnp.where(row <= col, 0.0, NEG)

    def run_branch(branch, k_ref_, vt_ref_, n_tiles, first_mask, later_mask):
        def keys(kt):
            return k_ref_[0, pl.ds(pl.multiple_of(kt * t, t), t), :]

        def values(kt):
            return jnp.concatenate([vt_ref_[0, 0, kt], ones_rows], axis=0)

        def tile_at(i):
            return qi - jnp.minimum(i, n_tiles - 1)

        k0 = keys(qi)
        k1 = keys(tile_at(1))
        m = []
        for p in range(hp):
            s = _dot_nt(k0, qh[p]) + (first_mask + alibi[p])
            m0 = _col_max(s)
            m.append(m0)
            p_ref[p] = jnp.exp(s - m0).astype(BF16)
            s_ref[p] = _dot_nt(k1, qh[p])
        acc_ref[...] = jnp.zeros_like(acc_ref)

        def trip(i, carry):
            a_old, m_old = carry[0:hp], carry[hp:]
            kt = qi - i
            v_aug = values(kt + 1)
            k_nxt = keys(tile_at(i + 1))
            a_new, m_new = [], []
            for p in range(hp):
                acc_ref[p] = a_old[p] * acc_ref[p] + _dot(v_aug, p_ref[p])
                mx, ax, px = _softmax_step(s_ref[p] + alibi[p] + later_mask(i, kt, p), m_old[p])
                m_new.append(mx)
                a_new.append(ax)
                p_ref[p] = px
                s_ref[p] = _dot_nt(k_nxt, qh[p])
            return (*a_new, *m_new)

        one = jnp.ones((1, t), F32)
        carry = lax.fori_loop(1, n_tiles, trip, (*([one] * hp), *m))
        v_aug = values(qi - (n_tiles - 1))
        for p in range(hp):
            acc = carry[p] * acc_ref[p] + _dot(v_aug, p_ref[p])
            scale = gates[branch * hp + p:branch * hp + p + 1, :] * (1.0 / acc[dh:dh + 1, :])
            out_ref[p * dh:(p + 1) * dh, :] += scale * acc[:dh, :]

    blocks_per_tile = t // SEL_BLOCK

    def sel_rows(kt, shift):
        return jnp.concatenate(
            [jnp.broadcast_to(sel_ref[pl.ds(kt * blocks_per_tile + i, 1), :] + shift, (SEL_BLOCK, t))
             for i in range(blocks_per_tile)], axis=0)

    run_branch(1, ks_ref, vst_ref, qi + 1, sel_rows(qi, 0.0) + causal_add,
               lambda i, kt, p: sel_rows(kt, slopes[p] * ((kt - qi) * t).astype(F32)))

    far_add = jnp.where(row > col, 0.0, NEG)
    run_branch(2, kw_ref, vwt_ref, jnp.minimum(qi, WINDOW // t) + 1, causal_add,
               lambda i, kt, p: jnp.where(i == WINDOW // t, far_add, 0.0) + slopes[p] * ((kt - qi) * t).astype(F32))

    o_ref[0] = out_ref[...].T.astype(o_ref.dtype)


def nsa_attention(q, glt, k_cmp, v_cmp_t, ks4, vs_t, kw4, vw_t, slopes):
    b, s, d = q.shape
    g = NSA_GROUPS
    t = ATT_T
    assert WINDOW == 2 * t
    n_cmp = k_cmp.shape[2]
    n_sel = s // SEL_BLOCK
    j = jnp.arange(n_sel)[:, None]
    n = jnp.arange(n_cmp)[None, :]
    overlap_t = ((n * CMP_STRIDE < (j + 1) * SEL_BLOCK) & (n * CMP_STRIDE + CMP_BLOCK - 1 >= j * SEL_BLOCK)
                 & (n < n_cmp - 1)).astype(BF16)
    k_spec = pl.BlockSpec((1, s, NSA_GW), lambda i, gg, qq: (i, 0, gg))
    vt_spec = pl.BlockSpec((1, 1, s // t, HEAD_DIM, t), lambda i, gg, qq: (i, gg, 0, 0, 0))
    q_spec = pl.BlockSpec((1, t, NSA_GW), lambda i, gg, qq: (i, qq, gg))
    return pl.pallas_call(
        functools.partial(_nsa_kernel, top_n=min(SEL_TOPN, n_sel)),
        grid=(b, g, s // t),
        in_specs=[pl.BlockSpec(memory_space=pltpu.SMEM), q_spec,
                  pl.BlockSpec((1, 1, N_GATE_ROWS, t), lambda i, gg, qq: (i, gg, 0, qq)),
                  pl.BlockSpec((1, 1, n_cmp, NSA_GW), lambda i, gg, qq: (i, gg, 0, 0)),
                  pl.BlockSpec((1, 1, HEAD_DIM, n_cmp), lambda i, gg, qq: (i, gg, 0, 0)),
                  pl.BlockSpec(overlap_t.shape, lambda i, gg, qq: (0, 0)),
                  k_spec, vt_spec, k_spec, vt_spec],
        out_specs=q_spec,
        out_shape=jax.ShapeDtypeStruct((b, s, d), BF16),
        scratch_shapes=[pltpu.VMEM((n_sel, t), F32), pltpu.VMEM((NSA_HPG, 2 * HEAD_DIM, t), F32),
                        pltpu.VMEM((NSA_HPG, t, t), BF16), pltpu.VMEM((NSA_HPG, t, t), F32),
                        pltpu.VMEM((NSA_GW, t), F32)],
        compiler_params=_cparams("parallel", "parallel", "arbitrary"), name="nsa_attention",
    )(slopes, q, glt, k_cmp, v_cmp_t, overlap_t, ks4, vs_t, kw4, vw_t)


def _pair_queries(q_ref):
    q = q_ref[0] * jnp.asarray(HEAD_DIM ** -0.5, BF16)
    lane = _iota(q.shape, 1)
    zero = jnp.zeros_like(q)
    return [jnp.where(lane < HEAD_DIM, q, zero), jnp.where(lane >= HEAD_DIM, q, zero)]


def _pair_values(vt, x):
    r = _iota(vt.shape, 0)
    own = (r < HEAD_DIM) if x == 0 else (r >= HEAD_DIM)
    return jnp.where(own, vt, jnp.ones_like(vt))


def _moba_kernel(slopes_ref, q_ref, k_ref, vt_ref, o_ref, kmean_ref, sel_ref, acc_ref, p_ref, s_ref):
    hpair = pl.program_id(1)
    qi = pl.program_id(2)
    t = q_ref.shape[1]
    n_blk = k_ref.shape[1] // t
    dh = HEAD_DIM
    t0 = qi * t

    @pl.when(qi == 0)
    def _():
        for n in range(n_blk):
            kmean_ref[n:n + 1, :] = jnp.mean(k_ref[0, n * t:(n + 1) * t, :].astype(F32), axis=0, keepdims=True)

    qh = _pair_queries(q_ref)
    slopes = [slopes_ref[2 * hpair], slopes_ref[2 * hpair + 1]]
    row = _iota((t, t), 0)
    col = _iota((t, t), 1)
    alibi = [slopes[x] * row.astype(F32) for x in range(2)]

    km_hi, km_lo = _split(kmean_ref[...])
    past = _iota((n_blk, t), 0) < qi
    k_own = k_ref[0, pl.ds(pl.multiple_of(t0, t), t), :]
    k_first = k_ref[0, 0:t, :]
    m = []
    for x in range(2):
        gate = _dot_nt(km_hi, qh[x]) + _dot_nt(km_lo, qh[x])
        sel = _pick_top(jnp.where(past, gate, NEG), min(MOBA_TOPK, n_blk))
        sel_ref[x] = jnp.where(past, sel, NEG)
        s = jnp.where(row <= col, _dot_nt(k_own, qh[x]) + alibi[x], NEG)
        m0 = _col_max(s)
        m.append(m0)
        p_ref[x] = jnp.exp(s - m0).astype(BF16)
        s_ref[x] = _dot_nt(k_first, qh[x])
    acc_ref[...] = jnp.zeros_like(acc_ref)

    def past_block(n, carry):
        a_old, v_idx, m_old = carry[0:2], carry[2], carry[3:5]
        vt = vt_ref[0, 0, v_idx]
        k_nxt = k_ref[0, pl.ds(pl.multiple_of(jnp.minimum(n + 1, n_blk - 1) * t, t), t), :]
        off = ((n - qi) * t).astype(F32)
        a_new, m_new = [], []
        for x in range(2):
            acc_ref[x] = a_old[x] * acc_ref[x] + _dot(_pair_values(vt, x), p_ref[x])
            s = s_ref[x] + alibi[x] + (sel_ref[x, pl.ds(n, 1), :] + slopes[x] * off)
            mx, ax, px = _softmax_step(s, m_old[x])
            m_new.append(mx)
            a_new.append(ax)
            p_ref[x] = px
            s_ref[x] = _dot_nt(k_nxt, qh[x])
        return (*a_new, n, *m_new)

    one = jnp.ones((1, t), F32)
    carry = lax.fori_loop(0, qi, past_block, (one, one, qi, *m))
    vt = vt_ref[0, 0, carry[2]]
    acc_a, acc_b = (carry[x] * acc_ref[x] + _dot(_pair_values(vt, x), p_ref[x]) for x in range(2))
    o_t = jnp.concatenate([acc_a[:dh] * (1.0 / acc_a[dh:dh + 1]), acc_b[dh:] * (1.0 / acc_b[0:1])], axis=0)
    o_ref[0] = o_t.T.astype(o_ref.dtype)


def _pair_specs(s, t):
    pw = 2 * HEAD_DIM
    q_spec = pl.BlockSpec((1, t, pw), lambda i, h, qq: (i, qq, h))
    k_spec = pl.BlockSpec((1, s, pw), lambda i, h, qq: (i, 0, h))
    vt_spec = pl.BlockSpec((1, 1, s // t, pw, t), lambda i, h, qq: (i, h, 0, 0, 0))
    return q_spec, k_spec, vt_spec


def _pair_value_tiles(v, t):
    b, s, d = v.shape
    pw = 2 * HEAD_DIM
    return v.reshape(b, s // t, t, d // pw, pw).transpose(0, 3, 1, 4, 2)


def moba_attention(q, k, v, slopes):
    b, s, d = q.shape
    t = ATT_T
    pw = 2 * HEAD_DIM
    assert t == MOBA_BLOCK and s % MOBA_BLOCK == 0
    q_spec, k_spec, vt_spec = _pair_specs(s, t)
    return pl.pallas_call(
        _moba_kernel, grid=(b, N_HEADS // 2, s // t),
        in_specs=[pl.BlockSpec(memory_space=pltpu.SMEM), q_spec, k_spec, vt_spec],
        out_specs=q_spec,
        out_shape=jax.ShapeDtypeStruct((b, s, d), BF16),
        scratch_shapes=[pltpu.VMEM((s // t, pw), F32), pltpu.VMEM((2, s // t, t), F32),
                        pltpu.VMEM((2, pw, t), F32), pltpu.VMEM((2, t, t), BF16), pltpu.VMEM((2, t, t), F32)],
        compiler_params=_cparams("parallel", "parallel", "arbitrary"), name="moba_attention",
    )(slopes, q, k, _pair_value_tiles(v, t))


def _sb_kernel(q_ref, k_ref, vt_ref, tri_ref, o_ref, acc_ref, a_ref, z_ref):
    qi = pl.program_id(2)
    t = q_ref.shape[1]
    qh = _pair_queries(q_ref)
    row = _iota((t, t), 0)
    col = _iota((t, t), 1)
    tri = tri_ref[...]

    def keys(j):
        return k_ref[0, pl.ds(pl.multiple_of(j * t, t), t), :]

    def log_complement(z):
        return -(jnp.maximum(z, 0.0) + jnp.log(1.0 + jnp.exp(-jnp.abs(z))))

    def suffix_sums(lg, carry):
        l_hi, l_lo = _split(lg)
        return _dot(tri, l_hi) + _dot(tri, l_lo) + carry

    acc_ref[...] = jnp.zeros_like(acc_ref)
    k_diag = keys(qi)
    k_nxt = keys(jnp.maximum(qi - 1, 0))
    causal = row < col
    carry = []
    for x in range(2):
        z = _dot_nt(k_diag, qh[x])
        c = suffix_sums(jnp.where(causal, log_complement(z), 0.0), 0.0)
        a_ref[x] = jnp.where(causal, jnp.exp(z + c), 0.0).astype(BF16)
        z_ref[x] = _dot_nt(k_nxt, qh[x])
        carry.append(c[0:1, :])

    def tile(it, carry):
        j = qi - it
        vt_prev = vt_ref[0, 0, j + 1]
        k_nxt = keys(jnp.maximum(j - 1, 0))
        new = []
        for x in range(2):
            acc_ref[x] += _dot(vt_prev, a_ref[x])
            z = z_ref[x]
            c = suffix_sums(log_complement(z), carry[x])
            a_ref[x] = jnp.exp(z + c).astype(BF16)
            z_ref[x] = _dot_nt(k_nxt, qh[x])
            new.append(c[0:1, :])
        return tuple(new)

    lax.fori_loop(1, qi + 1, tile, tuple(carry))
    vt_last = vt_ref[0, 0, 0]
    r = _iota((2 * HEAD_DIM, t), 0)
    o_t = jnp.where(r < HEAD_DIM, acc_ref[0] + _dot(vt_last, a_ref[0]), acc_ref[1] + _dot(vt_last, a_ref[1]))
    o_ref[0] = o_t.T.astype(o_ref.dtype)


def sb_attention(q, k, v):
    b, s, d = q.shape
    t = ATT_T
    pw = 2 * HEAD_DIM
    r = jnp.arange(t)
    tri = (r[None, :] >= r[:, None]).astype(BF16)
    q_spec, k_spec, vt_spec = _pair_specs(s, t)
    return pl.pallas_call(
        _sb_kernel, grid=(b, N_HEADS // 2, s // t),
        in_specs=[q_spec, k_spec, vt_spec, pl.BlockSpec((t, t), lambda i, h, qq: (0, 0))],
        out_specs=q_spec,
        out_shape=jax.ShapeDtypeStruct((b, s, d), BF16),
        scratch_shapes=[pltpu.VMEM((2, pw, t), F32), pltpu.VMEM((2, t, t), BF16), pltpu.VMEM((2, t, t), F32)],
        compiler_params=_cparams("parallel", "parallel", "arbitrary"), name="sb_attention",
    )(q, k, _pair_value_tiles(v, t), tri)


def _router_kernel(x_ref, g_ref, r_ref, o_ref):
    h = _rms(x_ref[...], g_ref[...])
    h_hi, h_lo = _split(h)
    r_hi, r_lo = _split(r_ref[...])
    logits = _dot(h_hi, r_hi) + _dot(h_hi, r_lo) + _dot(h_lo, r_hi)
    lane = _iota(logits.shape, 1)
    lane_f = lane.astype(F32)
    x = jnp.where(lane < N_EXPERTS, logits, NEG)
    m1, i1 = _first_argmax(x, lane_f, 1)
    m2, i2 = _first_argmax(jnp.where(lane_f == i1, PICKED, x), lane_f, 1)
    e = jnp.exp(m2 - m1)
    w1 = 1.0 / (1.0 + e)
    w2 = e * w1
    o_ref[...] = jnp.where(lane == 0, i1, jnp.where(lane == 1, i2, jnp.where(lane == 2, w1, jnp.where(lane == 3, w2, 0.0))))


def moe_router(x2, g, router):
    n_tok, d = x2.shape
    tm = min(PROJ_TM, n_tok)
    r_pad = jnp.pad(router, ((0, 0), (0, LANES - router.shape[1])))
    return pl.pallas_call(
        _router_kernel, grid=(n_tok // tm,),
        in_specs=[pl.BlockSpec((tm, d), lambda i: (i, 0)), pl.BlockSpec((1, d), lambda i: (0, 0)),
                  pl.BlockSpec((d, LANES), lambda i: (0, 0))],
        out_specs=pl.BlockSpec((tm, LANES), lambda i: (i, 0)),
        out_shape=jax.ShapeDtypeStruct((n_tok, LANES), F32),
        compiler_params=_cparams("parallel"), name="moe_router",
    )(x2, g.reshape(1, d), r_pad)


def _row_copy(src_ref, dst_ref, src_row, dst_row, sem):
    return pltpu.make_async_copy(src_ref.at[pl.ds(src_row, 1)], dst_ref.at[pl.ds(dst_row, 1)], sem)


def _gather_kernel(idx_ref, src_ref, o_ref, sem):
    rows = o_ref.shape[0]

    def start(r, c):
        _row_copy(src_ref, o_ref, idx_ref[0, 0, r], r, sem).start()
        return c

    def wait(r, c):
        _row_copy(src_ref, o_ref, 0, r, sem).wait()
        return c

    lax.fori_loop(0, rows, start, 0)
    lax.fori_loop(0, rows, wait, 0)


def gather_rows(src, idx):
    n = idx.shape[0]
    d = src.shape[1]
    r = GATHER_R
    return pl.pallas_call(
        _gather_kernel, grid=(n // r,),
        in_specs=[pl.BlockSpec((1, 1, r), lambda i: (i, 0, 0), memory_space=pltpu.SMEM),
                  pl.BlockSpec(memory_space=pl.ANY)],
        out_specs=pl.BlockSpec((r, d), lambda i: (i, 0)),
        out_shape=jax.ShapeDtypeStruct((n, d), src.dtype),
        scratch_shapes=[pltpu.SemaphoreType.DMA(())],
        compiler_params=_cparams("arbitrary"), name="moe_dispatch",
    )(idx.reshape(n // r, 1, r), src)


def _moe_ffn_kernel(te_ref, nu_ref, x_ref, ws_ref, g_ref, wg_ref, wu_ref, wd_ref, o_ref, h_ref, acc_ref):
    i = pl.program_id(0)
    j = pl.program_id(1)
    used = i < nu_ref[0]

    @pl.when(j == 0)
    def _():
        h_ref[...] = _rms(x_ref[...], g_ref[...]).astype(BF16)
        acc_ref[...] = jnp.zeros_like(acc_ref)

    @pl.when(used)
    def _():
        h = h_ref[...]
        a = _dot(h, wg_ref[0])
        u = _dot(h, wu_ref[0])
        act = (a * _sigmoid(a) * u).astype(BF16)
        acc_ref[...] += _dot(act, wd_ref[0])

    @pl.when(j == pl.num_programs(1) - 1)
    def _():
        o_ref[...] = acc_ref[...] * ws_ref[...]


def moe_ffn(x_sorted, w_slot, g, wg, wu, wd, tile_expert, n_used):
    n_slots, d = x_sorted.shape
    f = wg.shape[2]
    tm, tf = MOE_TM, FFN_TF
    nf = f // tf

    def fidx(i, j, nu):
        return jnp.where(i < nu[0], j, nf - 1)

    grid_spec = pltpu.PrefetchScalarGridSpec(
        num_scalar_prefetch=2, grid=(n_slots // tm, nf),
        in_specs=[pl.BlockSpec((tm, d), lambda i, j, te, nu: (i, 0)),
                  pl.BlockSpec((tm, 1), lambda i, j, te, nu: (i, 0)),
                  pl.BlockSpec((1, d), lambda i, j, te, nu: (0, 0)),
                  pl.BlockSpec((1, d, tf), lambda i, j, te, nu: (te[i], 0, fidx(i, j, nu))),
                  pl.BlockSpec((1, d, tf), lambda i, j, te, nu: (te[i], 0, fidx(i, j, nu))),
                  pl.BlockSpec((1, tf, d), lambda i, j, te, nu: (te[i], fidx(i, j, nu), 0))],
        out_specs=pl.BlockSpec((tm, d), lambda i, j, te, nu: (i, 0)),
        scratch_shapes=[pltpu.VMEM((tm, d), BF16), pltpu.VMEM((tm, d), F32)])
    return pl.pallas_call(
        _moe_ffn_kernel, grid_spec=grid_spec,
        out_shape=jax.ShapeDtypeStruct((n_slots, d), F32),
        compiler_params=_cparams("arbitrary", "arbitrary"), name="moe_ffn",
    )(tile_expert, n_used, x_sorted, w_slot, g.reshape(1, d), wg, wu, wd)


def _combine_kernel(s1_ref, s2_ref, x_ref, y_ref, g_ref, o_ref, b1_ref, b2_ref, sem, *, final_norm):
    rows = o_ref.shape[0]

    def start(r, c):
        _row_copy(y_ref, b1_ref, s1_ref[0, 0, r], r, sem).start()
        _row_copy(y_ref, b2_ref, s2_ref[0, 0, r], r, sem).start()
        return c

    def wait(r, c):
        _row_copy(y_ref, b1_ref, 0, r, sem).wait()
        _row_copy(y_ref, b2_ref, 0, r, sem).wait()
        return c

    lax.fori_loop(0, rows, start, 0)
    lax.fori_loop(0, rows, wait, 0)
    out = x_ref[...] + b1_ref[...] + b2_ref[...]
    if final_norm:
        out = _rms(out, g_ref[...])
    o_ref[...] = out


def moe_combine(x2, y, slot1, slot2, final_g):
    n_tok, d = x2.shape
    r = min(GATHER_R, n_tok)
    final_norm = final_g is not None
    g = final_g if final_norm else jnp.ones((d,), F32)
    idx_spec = pl.BlockSpec((1, 1, r), lambda i: (i, 0, 0), memory_space=pltpu.SMEM)
    return pl.pallas_call(
        functools.partial(_combine_kernel, final_norm=final_norm), grid=(n_tok // r,),
        in_specs=[idx_spec, idx_spec, pl.BlockSpec((r, d), lambda i: (i, 0)),
                  pl.BlockSpec(memory_space=pl.ANY), pl.BlockSpec((1, d), lambda i: (0, 0))],
        out_specs=pl.BlockSpec((r, d), lambda i: (i, 0)),
        out_shape=jax.ShapeDtypeStruct((n_tok, d), F32),
        scratch_shapes=[pltpu.VMEM((r, d), F32), pltpu.VMEM((r, d), F32), pltpu.SemaphoreType.DMA(())],
        compiler_params=_cparams("arbitrary"), name="moe_combine",
    )(slot1.reshape(n_tok // r, 1, r), slot2.reshape(n_tok // r, 1, r), x2, y, g.reshape(1, d))


def moe_layer(x2, g, router, wg, wu, wd, final_g=None):
    n_tok, d = x2.shape
    tm = MOE_TM
    n_e = router.shape[1]
    info = moe_router(x2, g, router)
    expert = info[:, :2].astype(jnp.int32).reshape(-1)
    weight = info[:, 2:4].reshape(-1)
    onehot = (expert[:, None] == jnp.arange(n_e)[None, :]).astype(jnp.int32)
    csum = jnp.cumsum(onehot, axis=0)
    rank = jnp.take_along_axis(csum, expert[:, None], axis=1)[:, 0] - 1
    padded = ((csum[-1] + tm - 1) // tm) * tm
    ends = jnp.cumsum(padded)
    slot = (ends - padded)[expert] + rank
    n_slots = 2 * n_tok + n_e * tm
    n_tiles = n_slots // tm
    token_of_slot = jnp.zeros((n_slots,), jnp.int32).at[slot].set(jnp.arange(2 * n_tok, dtype=jnp.int32) // 2)
    weight_of_slot = jnp.zeros((n_slots,), F32).at[slot].set(weight)
    n_used = (ends[-1] // tm).astype(jnp.int32)
    tile_start = jnp.minimum(jnp.arange(n_tiles, dtype=jnp.int32), n_used - 1) * tm
    tile_expert = jnp.minimum(jnp.sum((ends[None, :] <= tile_start[:, None]).astype(jnp.int32), axis=1), n_e - 1)

    x_sorted = gather_rows(x2, token_of_slot)
    y = moe_ffn(x_sorted, weight_of_slot.reshape(n_slots, 1), g, wg, wu, wd, tile_expert, n_used.reshape(1))
    slot2 = slot.reshape(n_tok, 2)
    return moe_combine(x2, y, slot2[:, 0], slot2[:, 1], final_g)


def _alibi_slopes():
    return 2.0 ** (-8.0 * jnp.arange(1, N_HEADS + 1, dtype=F32) / N_HEADS)


def nsa_layer(x, g, w_in, cmpk, cmpv, w_out):
    b, s, d = x.shape
    n_tok = b * s
    x2 = x.reshape(n_tok, d)
    grp, hpg, dh, t = NSA_GROUPS, NSA_HPG, HEAD_DIM, ATT_T
    kvd = grp * dh
    w_q = w_in[:, :d].astype(BF16)
    w_kv = w_in[:, d:d + 6 * kvd].astype(BF16)
    w_g = jnp.pad(w_in[:, d + 6 * kvd:], ((0, 0), (0, LANES - 3 * N_HEADS))).astype(BF16)
    q, kv, gl = norm_proj(x2, g, [w_q, w_kv, w_g], [BF16, BF16, F32])
    kc, vc, ks, vs, kw, vw = (kv[:, i * kvd:(i + 1) * kvd] for i in range(6))

    def chunks(a):
        a = a.reshape(b, s // CMP_STRIDE, CMP_STRIDE, grp, dh).transpose(0, 3, 1, 2, 4)
        return a.reshape(b, grp, s // CMP_STRIDE, CMP_STRIDE * dh)

    def rep(a):
        a = jnp.broadcast_to(a.reshape(b, s, grp, 1, dh), (b, s, grp, hpg, dh))
        return a.reshape(b, s, grp * hpg * dh)

    def tiles_t(a):
        return a.reshape(b, s // t, t, grp, dh).transpose(0, 3, 1, 4, 2)

    k_cmp = nsa_compress(chunks(kc), *cmpk, transposed=False)
    v_cmp_t = nsa_compress(chunks(vc), *cmpv, transposed=True)
    glt = gl[:, :3 * N_HEADS].reshape(b, s, 3, grp, hpg).transpose(0, 3, 2, 4, 1).reshape(b, grp, 3 * hpg, s)
    glt = jnp.pad(glt, ((0, 0), (0, 0), (0, N_GATE_ROWS - 3 * hpg), (0, 0)))
    o = nsa_attention(q.reshape(b, s, d), glt, k_cmp, v_cmp_t, rep(ks), tiles_t(vs), rep(kw), tiles_t(vw),
                      _alibi_slopes())
    return matmul_residual(o.reshape(n_tok, d), w_out.astype(BF16), x2).reshape(b, s, d)


def _qkv_layer(x, g, w_in, w_out, attend):
    b, s, d = x.shape
    n_tok = b * s
    x2 = x.reshape(n_tok, d)
    ws = [w_in[:, i * d:(i + 1) * d].astype(BF16) for i in range(3)]
    q, k, v = (a.reshape(b, s, d) for a in norm_proj(x2, g, ws, [BF16] * 3))
    o = attend(q, k, v)
    return matmul_residual(o.reshape(n_tok, d), w_out.astype(BF16), x2).reshape(b, s, d)


def moba_layer(x, g, w_in, w_out):
    slopes = _alibi_slopes()
    return _qkv_layer(x, g, w_in, w_out, lambda q, k, v: moba_attention(q, k, v, slopes))


def sb_layer(x, g, w_in, w_out):
    return _qkv_layer(x, g, w_in, w_out, sb_attention)


def dense_ffn_layer(x, g, wg, wu, wd):
    b, s, d = x.shape
    return dense_ffn(x.reshape(b * s, d), g, wg.astype(BF16), wu.astype(BF16), wd.astype(BF16)).reshape(b, s, d)


def moe_ffn_layer(x, g, router, wg, wu, wd, final_g=None):
    b, s, d = x.shape
    return moe_layer(x.reshape(b * s, d), g, router, wg.astype(BF16), wu.astype(BF16), wd.astype(BF16),
                     final_g).reshape(b, s, d)


def kernel(x, l0_attn_norm, l0_nsa_w_in, l0_nsa_cmpk_pos, l0_nsa_cmpk_w1, l0_nsa_cmpk_w2, l0_nsa_cmpv_pos, l0_nsa_cmpv_w1, l0_nsa_cmpv_w2, l0_nsa_w_out, l0_ffn_norm, l0_ffn_w_gate, l0_ffn_w_up, l0_ffn_w_down, l1_attn_norm, l1_moba_w_in, l1_moba_w_out, l1_ffn_norm, l1_moe_router, l1_moe_w_gate, l1_moe_w_up, l1_moe_w_down, l2_attn_norm, l2_sb_w_in, l2_sb_w_out, l2_ffn_norm, l2_ffn_w_gate, l2_ffn_w_up, l2_ffn_w_down, l3_attn_norm, l3_nsa_w_in, l3_nsa_cmpk_pos, l3_nsa_cmpk_w1, l3_nsa_cmpk_w2, l3_nsa_cmpv_pos, l3_nsa_cmpv_w1, l3_nsa_cmpv_w2, l3_nsa_w_out, l3_ffn_norm, l3_moe_router, l3_moe_w_gate, l3_moe_w_up, l3_moe_w_down, final_norm):
    x = nsa_layer(x, l0_attn_norm, l0_nsa_w_in, (l0_nsa_cmpk_pos, l0_nsa_cmpk_w1, l0_nsa_cmpk_w2),
                  (l0_nsa_cmpv_pos, l0_nsa_cmpv_w1, l0_nsa_cmpv_w2), l0_nsa_w_out)
    x = dense_ffn_layer(x, l0_ffn_norm, l0_ffn_w_gate, l0_ffn_w_up, l0_ffn_w_down)
    x = moba_layer(x, l1_attn_norm, l1_moba_w_in, l1_moba_w_out)
    x = moe_ffn_layer(x, l1_ffn_norm, l1_moe_router, l1_moe_w_gate, l1_moe_w_up, l1_moe_w_down)
    x = sb_layer(x, l2_attn_norm, l2_sb_w_in, l2_sb_w_out)
    x = dense_ffn_layer(x, l2_ffn_norm, l2_ffn_w_gate, l2_ffn_w_up, l2_ffn_w_down)
    x = nsa_layer(x, l3_attn_norm, l3_nsa_w_in, (l3_nsa_cmpk_pos, l3_nsa_cmpk_w1, l3_nsa_cmpk_w2),
                  (l3_nsa_cmpv_pos, l3_nsa_cmpv_w1, l3_nsa_cmpv_w2), l3_nsa_w_out)
    return moe_ffn_layer(x, l3_ffn_norm, l3_moe_router, l3_moe_w_gate, l3_moe_w_up, l3_moe_w_down, final_norm)
```

```python
import functools

import jax
import jax.numpy as jnp
from jax import lax
from jax.experimental import pallas as pl
from jax.experimental.pallas import tpu as pltpu

F32 = jnp.float32
BF16 = jnp.bfloat16

N_HEADS = 16
HEAD_DIM = 64
LOG2_HEAD_DIM = 6
RMS_EPS = 1e-6
LANES = 128
NSA_GROUPS = 4
NSA_HPG = N_HEADS // NSA_GROUPS
NSA_GW = NSA_HPG * HEAD_DIM
CMP_BLOCK = 32
CMP_STRIDE = 16
CMP_HIDDEN = 2 * HEAD_DIM
SEL_BLOCK = 64
LOG2_SEL_BLOCK = 6
SEL_TOPN = 16
WINDOW = 512
MOBA_BLOCK = 256
MOBA_TOPK = 3
N_EXPERTS = 8
NEG = -1e30
PICKED = -3e38
VMEM_LIMIT = 52 * 1024 * 1024

ATT_T = 256
N_GATE_ROWS = 16
PROJ_TM = 512
FFN_TM = 512
FFN_TF = 512
MOE_TM = 512
DISPATCH_R = 1024
COMBINE_R = 512


def _cparams(*sem):
    return pltpu.CompilerParams(dimension_semantics=sem, vmem_limit_bytes=VMEM_LIMIT)


def _dot(a, b):
    return jnp.dot(a, b, preferred_element_type=F32)


def _dot_nt(a, b):
    return lax.dot_general(a, b, (((1,), (1,)), ((), ())), preferred_element_type=F32)


def _split(a):
    hi = a.astype(BF16)
    lo = (a - hi.astype(F32)).astype(BF16)
    return hi, lo


def _rms(x, g):
    ms = jnp.mean(x * x, axis=-1, keepdims=True)
    return x * lax.rsqrt(ms + RMS_EPS) * g


def _sigmoid(x):
    return 1.0 / (1.0 + jnp.exp(-x))


def _first_argmax(x, idx_f, axis):
    m = jnp.max(x, axis=axis, keepdims=True)
    first = jnp.min(jnp.where(x == m, idx_f, 1e9), axis=axis, keepdims=True)
    return m, first


def _iota(shape, dim):
    return lax.broadcasted_iota(jnp.int32, shape, dim)


def _norm_proj_kernel(x_ref, g_ref, *refs):
    n = len(refs) // 2
    h = _rms(x_ref[...], g_ref[...]).astype(BF16)
    for w_ref, o_ref in zip(refs[:n], refs[n:]):
        o_ref[...] = _dot(h, w_ref[...]).astype(o_ref.dtype)


def norm_proj(x2, g, ws, out_dtypes):
    n_tok, d = x2.shape
    tm = min(PROJ_TM, n_tok)
    in_specs = [pl.BlockSpec((tm, d), lambda i: (i, 0)), pl.BlockSpec((1, d), lambda i: (0, 0))]
    in_specs += [pl.BlockSpec(w.shape, lambda i: (0, 0)) for w in ws]
    out_specs = [pl.BlockSpec((tm, w.shape[1]), lambda i: (i, 0)) for w in ws]
    out_shape = [jax.ShapeDtypeStruct((n_tok, w.shape[1]), dt) for w, dt in zip(ws, out_dtypes)]
    return pl.pallas_call(
        _norm_proj_kernel, grid=(n_tok // tm,), in_specs=in_specs, out_specs=out_specs,
        out_shape=out_shape, compiler_params=_cparams("parallel"), name="norm_proj",
    )(x2, g.reshape(1, d), *ws)


def _mm_res_kernel(a_ref, w_ref, r_ref, o_ref):
    o_ref[...] = r_ref[...] + _dot(a_ref[...], w_ref[...])


def matmul_residual(a, w, res):
    n_tok, k = a.shape
    d = w.shape[1]
    tm = min(PROJ_TM, n_tok)
    return pl.pallas_call(
        _mm_res_kernel, grid=(n_tok // tm,),
        in_specs=[pl.BlockSpec((tm, k), lambda i: (i, 0)), pl.BlockSpec((k, d), lambda i: (0, 0)),
                  pl.BlockSpec((tm, d), lambda i: (i, 0))],
        out_specs=pl.BlockSpec((tm, d), lambda i: (i, 0)),
        out_shape=jax.ShapeDtypeStruct((n_tok, d), F32),
        compiler_params=_cparams("parallel"), name="out_proj",
    )(a, w, res)


def _ffn_kernel(x_ref, g_ref, wg_ref, wu_ref, wd_ref, o_ref, h_ref, acc_ref):
    j = pl.program_id(1)

    @pl.when(j == 0)
    def _():
        h_ref[...] = _rms(x_ref[...], g_ref[...]).astype(BF16)
        acc_ref[...] = jnp.zeros_like(acc_ref)

    h = h_ref[...]
    a = _dot(h, wg_ref[...])
    u = _dot(h, wu_ref[...])
    act = (a * _sigmoid(a) * u).astype(BF16)
    acc_ref[...] += _dot(act, wd_ref[...])

    @pl.when(j == pl.num_programs(1) - 1)
    def _():
        o_ref[...] = x_ref[...] + acc_ref[...]


def dense_ffn(x2, g, wg, wu, wd):
    n_tok, d = x2.shape
    f = wg.shape[1]
    tm = min(FFN_TM, n_tok)
    tf = FFN_TF
    return pl.pallas_call(
        _ffn_kernel, grid=(n_tok // tm, f // tf),
        in_specs=[pl.BlockSpec((tm, d), lambda i, j: (i, 0)), pl.BlockSpec((1, d), lambda i, j: (0, 0)),
                  pl.BlockSpec((d, tf), lambda i, j: (0, j)), pl.BlockSpec((d, tf), lambda i, j: (0, j)),
                  pl.BlockSpec((tf, d), lambda i, j: (j, 0))],
        out_specs=pl.BlockSpec((tm, d), lambda i, j: (i, 0)),
        out_shape=jax.ShapeDtypeStruct((n_tok, d), F32),
        scratch_shapes=[pltpu.VMEM((tm, d), BF16), pltpu.VMEM((tm, d), F32)],
        compiler_params=_cparams("parallel", "arbitrary"), name="dense_ffn",
    )(x2, g.reshape(1, d), wg, wu, wd)


def _compress_kernel(c_ref, w1_ref, pos_ref, w2_ref, o_ref, *, transposed):
    c = c_ref[0, 0]
    n_chunks, cw = c.shape
    w1 = w1_ref[...]
    w1a = w1[:cw].astype(BF16)
    w1b = w1[cw:].astype(BF16)
    ph, plo = _split(pos_ref[...])
    wh, wlo = _split(w1)
    c0 = (_dot(ph, wh) + _dot(ph, wlo) + _dot(plo, wh))[0:1]
    a = _dot(c, w1a)
    b = _dot(c, w1b)
    hid = a + pltpu.roll(b, n_chunks - 1, 0) + c0
    act = 0.5 * hid * (1.0 + jnp.tanh(0.7978845608028654 * (hid + 0.044715 * hid * hid * hid)))
    act = act.astype(BF16)
    if transposed:
        o_ref[0, 0] = _dot_nt(w2_ref[...], act).astype(o_ref.dtype)
    else:
        o_ref[0, 0] = _dot(act, w2_ref[...]).astype(o_ref.dtype)


def nsa_compress(chunks, pos, w1, w2, transposed):
    b, g, n_chunks, cw = chunks.shape
    w1f = w1.reshape(CMP_BLOCK * HEAD_DIM, CMP_HIDDEN)
    pos8 = jnp.broadcast_to(pos.reshape(1, CMP_BLOCK * HEAD_DIM), (8, CMP_BLOCK * HEAD_DIM))
    if transposed:
        w2m = w2.T.astype(BF16)
        out_block = (1, 1, HEAD_DIM, n_chunks)
    else:
        w2m = jnp.tile(w2, (1, NSA_HPG)).astype(BF16)
        out_block = (1, 1, n_chunks, NSA_GW)
    return pl.pallas_call(
        functools.partial(_compress_kernel, transposed=transposed), grid=(b, g),
        in_specs=[pl.BlockSpec((1, 1, n_chunks, cw), lambda i, j: (i, j, 0, 0)),
                  pl.BlockSpec(w1f.shape, lambda i, j: (0, 0)),
                  pl.BlockSpec(pos8.shape, lambda i, j: (0, 0)),
                  pl.BlockSpec(w2m.shape, lambda i, j: (0, 0))],
        out_specs=pl.BlockSpec(out_block, lambda i, j: (i, j, 0, 0)),
        out_shape=jax.ShapeDtypeStruct((b, g) + out_block[2:], BF16),
        compiler_params=_cparams("parallel", "parallel"), name="nsa_compress",
    )(chunks, w1f, pos8, w2m)


def _col_max(s):
    k, t = s.shape
    return jnp.max(jnp.max(s.reshape(k // 32, 32, t), axis=0), axis=0, keepdims=True)


def _softmax_step(s, m_old):
    m_new = jnp.maximum(m_old, _col_max(s))
    return m_new, jnp.exp(m_old - m_new), jnp.exp(s - m_new).astype(BF16)


def _pick_top(x0, n_pick):
    idx_f = _iota(x0.shape, 0).astype(F32)

    def pick(_, carry):
        x, sel = carry
        _, first = _first_argmax(x, idx_f, 0)
        hit = idx_f == first
        return jnp.where(hit, PICKED, x), jnp.where(hit, 0.0, sel)

    return lax.fori_loop(0, n_pick, pick, (x0, jnp.full(x0.shape, NEG, F32)))[1]


def _nsa_kernel(slopes_ref, q_ref, glt_ref, kc_ref, vct_ref, ovt_ref, ks_ref, vst_ref, kw_ref, vwt_ref,
                o_ref, sel_ref, acc_ref, p_ref, s_ref, out_ref, *, top_n):
    g = pl.program_id(1)
    qi = pl.program_id(2)
    t = q_ref.shape[1]
    n_cmp = kc_ref.shape[2]
    hp = NSA_HPG
    dh = HEAD_DIM
    t0 = qi * t
    slopes = [slopes_ref[g * hp + p] for p in range(hp)]

    q = q_ref[0] * jnp.asarray(dh ** -0.5, BF16)
    slot = _iota(q.shape, 1) >> LOG2_HEAD_DIM
    qh = [jnp.where(slot == p, q, jnp.zeros_like(q)) for p in range(hp)]
    gates = _sigmoid(glt_ref[0, 0])

    row = _iota((t, t), 0)
    col = _iota((t, t), 1)
    row_f = row.astype(F32)
    ones_rows = jnp.ones((dh, t), BF16)


    n_row = _iota((n_cmp, t), 0)
    valid_c = (n_row * CMP_STRIDE + (CMP_BLOCK - 1)) <= (t0 + _iota((n_cmp, t), 1))
    cpos = (n_row * CMP_STRIDE - t0).astype(F32) + 0.5 * (CMP_BLOCK - 1)
    kc = kc_ref[0, 0]
    vct = vct_ref[0, 0]
    p_sum = jnp.zeros((n_cmp, t), F32)
    for p in range(hp):
        sp = jnp.where(valid_c, _dot_nt(kc, qh[p]) + slopes[p] * cpos, NEG)
        m = jnp.max(sp, axis=0, keepdims=True)
        e = jnp.where(valid_c, jnp.exp(sp - m), 0.0)
        pc = e * (1.0 / jnp.maximum(jnp.sum(e, axis=0, keepdims=True), 1e-30))
        p_sum = p_sum + pc
        out_ref[p * dh:(p + 1) * dh, :] = gates[p:p + 1, :] * _dot(vct, pc.astype(BF16))

    ps_hi, ps_lo = _split(p_sum)
    imp = _dot(ovt_ref[...], ps_hi) + _dot(ovt_ref[...], ps_lo)
    j_idx = _iota(imp.shape, 0)
    cur = (t0 + _iota(imp.shape, 1)) >> LOG2_SEL_BLOCK
    forced = (j_idx == 0) | (j_idx == cur) | (j_idx == cur - 1)
    x0 = jnp.where(j_idx > cur, NEG, jnp.where(forced, -NEG, imp))
    sel_ref[...] = _pick_top(x0, top_n)

    alibi = [slopes[p] * row_f for p in range(hp)]
    causal_add = jnp.where(row <= col, 0.0, NEG)

    def run_branch(branch, k_ref_, vt_ref_, n_tiles, first_mask, later_mask):
        def keys(kt):
            return k_ref_[0, pl.ds(pl.multiple_of(kt * t, t), t), :]

        def values(kt):
            return jnp.concatenate([vt_ref_[0, 0, kt], ones_rows], axis=0)

        def tile_at(i):
            return qi - jnp.minimum(i, n_tiles - 1)

        k0 = keys(qi)
        k1 = keys(tile_at(1))
        m = []
        for p in range(hp):
            s = _dot_nt(k0, qh[p]) + (first_mask + alibi[p])
            m0 = _col_max(s)
            m.append(m0)
            p_ref[p] = jnp.exp(s - m0).astype(BF16)
            s_ref[p] = _dot_nt(k1, qh[p])
        acc_ref[...] = jnp.zeros_like(acc_ref)

        def trip(i, carry):
            a_old, m_old = carry[0:hp], carry[hp:]
            kt = qi - i
            v_aug = values(kt + 1)
            k_nxt = keys(tile_at(i + 1))
            a_new, m_new = [], []
            for p in range(hp):
                acc_ref[p] = a_old[p] * acc_ref[p] + _dot(v_aug, p_ref[p])
                mx, ax, px = _softmax_step(s_ref[p] + alibi[p] + later_mask(i, kt, p), m_old[p])
                m_new.append(mx)
                a_new.append(ax)
                p_ref[p] = px
                s_ref[p] = _dot_nt(k_nxt, qh[p])
            return (*a_new, *m_new)

        one = jnp.ones((1, t), F32)
        carry = lax.fori_loop(1, n_tiles, trip, (*([one] * hp), *m))
        v_aug = values(qi - (n_tiles - 1))
        for p in range(hp):
            acc = carry[p] * acc_ref[p] + _dot(v_aug, p_ref[p])
            scale = gates[branch * hp + p:branch * hp + p + 1, :] * (1.0 / acc[dh:dh + 1, :])
            out_ref[p * dh:(p + 1) * dh, :] += scale * acc[:dh, :]

    blocks_per_tile = t // SEL_BLOCK

    def sel_rows(kt, shift):
        return jnp.concatenate(
            [jnp.broadcast_to(sel_ref[pl.ds(kt * blocks_per_tile + i, 1), :] + shift, (SEL_BLOCK, t))
             for i in range(blocks_per_tile)], axis=0)

    run_branch(1, ks_ref, vst_ref, qi + 1, sel_rows(qi, 0.0) + causal_add,
               lambda i, kt, p: sel_rows(kt, slopes[p] * ((kt - qi) * t).astype(F32)))

    far_add = jnp.where(row > col, 0.0, NEG)
    run_branch(2, kw_ref, vwt_ref, jnp.minimum(qi, WINDOW // t) + 1, causal_add,
               lambda i, kt, p: jnp.where(i == WINDOW // t, far_add, 0.0) + slopes[p] * ((kt - qi) * t).astype(F32))

    o_ref[0] = out_ref[...].T.astype(o_ref.dtype)


def nsa_attention(q, glt, k_cmp, v_cmp_t, ks4, vs_t, kw4, vw_t, slopes):
    b, s, d = q.shape
    g = NSA_GROUPS
    t = ATT_T
    assert WINDOW == 2 * t
    n_cmp = k_cmp.shape[2]
    n_sel = s // SEL_BLOCK
    j = jnp.arange(n_sel)[:, None]
    n = jnp.arange(n_cmp)[None, :]
    overlap_t = ((n * CMP_STRIDE < (j + 1) * SEL_BLOCK) & (n * CMP_STRIDE + CMP_BLOCK - 1 >= j * SEL_BLOCK)
                 & (n < n_cmp - 1)).astype(BF16)
    k_spec = pl.BlockSpec((1, s, NSA_GW), lambda i, gg, qq: (i, 0, gg))
    vt_spec = pl.BlockSpec((1, 1, s // t, HEAD_DIM, t), lambda i, gg, qq: (i, gg, 0, 0, 0))
    q_spec = pl.BlockSpec((1, t, NSA_GW), lambda i, gg, qq: (i, qq, gg))
    return pl.pallas_call(
        functools.partial(_nsa_kernel, top_n=min(SEL_TOPN, n_sel)),
        grid=(b, g, s // t),
        in_specs=[pl.BlockSpec(memory_space=pltpu.SMEM), q_spec,
                  pl.BlockSpec((1, 1, N_GATE_ROWS, t), lambda i, gg, qq: (i, gg, 0, qq)),
                  pl.BlockSpec((1, 1, n_cmp, NSA_GW), lambda i, gg, qq: (i, gg, 0, 0)),
                  pl.BlockSpec((1, 1, HEAD_DIM, n_cmp), lambda i, gg, qq: (i, gg, 0, 0)),
                  pl.BlockSpec(overlap_t.shape, lambda i, gg, qq: (0, 0)),
                  k_spec, vt_spec, k_spec, vt_spec],
        out_specs=q_spec,
        out_shape=jax.ShapeDtypeStruct((b, s, d), BF16),
        scratch_shapes=[pltpu.VMEM((n_sel, t), F32), pltpu.VMEM((NSA_HPG, 2 * HEAD_DIM, t), F32),
                        pltpu.VMEM((NSA_HPG, t, t), BF16), pltpu.VMEM((NSA_HPG, t, t), F32),
                        pltpu.VMEM((NSA_GW, t), F32)],
        compiler_params=_cparams("parallel", "parallel", "arbitrary"), name="nsa_attention",
    )(slopes, q, glt, k_cmp, v_cmp_t, overlap_t, ks4, vs_t, kw4, vw_t)


def _pair_queries(q_ref):
    q = q_ref[0] * jnp.asarray(HEAD_DIM ** -0.5, BF16)
    lane = _iota(q.shape, 1)
    zero = jnp.zeros_like(q)
    return [jnp.where(lane < HEAD_DIM, q, zero), jnp.where(lane >= HEAD_DIM, q, zero)]


def _pair_values(vt, x):
    r = _iota(vt.shape, 0)
    own = (r < HEAD_DIM) if x == 0 else (r >= HEAD_DIM)
    return jnp.where(own, vt, jnp.ones_like(vt))


def _moba_kernel(slopes_ref, q_ref, k_ref, vt_ref, o_ref, kmean_ref, sel_ref, acc_ref, p_ref, s_ref):
    hpair = pl.program_id(1)
    qi = pl.program_id(2)
    t = q_ref.shape[1]
    n_blk = k_ref.shape[1] // t
    dh = HEAD_DIM
    t0 = qi * t

    @pl.when(qi == 0)
    def _():
        for n in range(n_blk):
            kmean_ref[n:n + 1, :] = jnp.mean(k_ref[0, n * t:(n + 1) * t, :].astype(F32), axis=0, keepdims=True)

    qh = _pair_queries(q_ref)
    slopes = [slopes_ref[2 * hpair], slopes_ref[2 * hpair + 1]]
    row = _iota((t, t), 0)
    col = _iota((t, t), 1)
    alibi = [slopes[x] * row.astype(F32) for x in range(2)]

    km_hi, km_lo = _split(kmean_ref[...])
    past = _iota((n_blk, t), 0) < qi
    k_own = k_ref[0, pl.ds(pl.multiple_of(t0, t), t), :]
    k_first = k_ref[0, 0:t, :]
    m = []
    for x in range(2):
        gate = _dot_nt(km_hi, qh[x]) + _dot_nt(km_lo, qh[x])
        sel = _pick_top(jnp.where(past, gate, NEG), min(MOBA_TOPK, n_blk))
        sel_ref[x] = jnp.where(past, sel, NEG)
        s = jnp.where(row <= col, _dot_nt(k_own, qh[x]) + alibi[x], NEG)
        m0 = _col_max(s)
        m.append(m0)
        p_ref[x] = jnp.exp(s - m0).astype(BF16)
        s_ref[x] = _dot_nt(k_first, qh[x])
    acc_ref[...] = jnp.zeros_like(acc_ref)

    def past_block(n, carry):
        a_old, v_idx, m_old = carry[0:2], carry[2], carry[3:5]
        vt = vt_ref[0, 0, v_idx]
        k_nxt = k_ref[0, pl.ds(pl.multiple_of(jnp.minimum(n + 1, n_blk - 1) * t, t), t), :]
        off = ((n - qi) * t).astype(F32)
        a_new, m_new = [], []
        for x in range(2):
            acc_ref[x] = a_old[x] * acc_ref[x] + _dot(_pair_values(vt, x), p_ref[x])
            s = s_ref[x] + alibi[x] + (sel_ref[x, pl.ds(n, 1), :] + slopes[x] * off)
            mx, ax, px = _softmax_step(s, m_old[x])
            m_new.append(mx)
            a_new.append(ax)
            p_ref[x] = px
            s_ref[x] = _dot_nt(k_nxt, qh[x])
        return (*a_new, n, *m_new)

    one = jnp.ones((1, t), F32)
    carry = lax.fori_loop(0, qi, past_block, (one, one, qi, *m))
    vt = vt_ref[0, 0, carry[2]]
    acc_a, acc_b = (carry[x] * acc_ref[x] + _dot(_pair_values(vt, x), p_ref[x]) for x in range(2))
    o_t = jnp.concatenate([acc_a[:dh] * (1.0 / acc_a[dh:dh + 1]), acc_b[dh:] * (1.0 / acc_b[0:1])], axis=0)
    o_ref[0] = o_t.T.astype(o_ref.dtype)


def _pair_specs(s, t):
    pw = 2 * HEAD_DIM
    q_spec = pl.BlockSpec((1, t, pw), lambda i, h, qq: (i, qq, h))
    k_spec = pl.BlockSpec((1, s, pw), lambda i, h, qq: (i, 0, h))
    vt_spec = pl.BlockSpec((1, 1, s // t, pw, t), lambda i, h, qq: (i, h, 0, 0, 0))
    return q_spec, k_spec, vt_spec


def _pair_value_tiles(v, t):
    b, s, d = v.shape
    pw = 2 * HEAD_DIM
    return v.reshape(b, s // t, t, d // pw, pw).transpose(0, 3, 1, 4, 2)


def moba_attention(q, k, v, slopes):
    b, s, d = q.shape
    t = ATT_T
    pw = 2 * HEAD_DIM
    assert t == MOBA_BLOCK and s % MOBA_BLOCK == 0
    q_spec, k_spec, vt_spec = _pair_specs(s, t)
    return pl.pallas_call(
        _moba_kernel, grid=(b, N_HEADS // 2, s // t),
        in_specs=[pl.BlockSpec(memory_space=pltpu.SMEM), q_spec, k_spec, vt_spec],
        out_specs=q_spec,
        out_shape=jax.ShapeDtypeStruct((b, s, d), BF16),
        scratch_shapes=[pltpu.VMEM((s // t, pw), F32), pltpu.VMEM((2, s // t, t), F32),
                        pltpu.VMEM((2, pw, t), F32), pltpu.VMEM((2, t, t), BF16), pltpu.VMEM((2, t, t), F32)],
        compiler_params=_cparams("parallel", "parallel", "arbitrary"), name="moba_attention",
    )(slopes, q, k, _pair_value_tiles(v, t))


def _sb_kernel(q_ref, k_ref, vt_ref, tri_ref, o_ref, acc_ref, a_ref, z_ref):
    qi = pl.program_id(2)
    t = q_ref.shape[1]
    qh = _pair_queries(q_ref)
    row = _iota((t, t), 0)
    col = _iota((t, t), 1)
    tri = tri_ref[...]

    def keys(j):
        return k_ref[0, pl.ds(pl.multiple_of(j * t, t), t), :]

    def log_complement(z):
        return -(jnp.maximum(z, 0.0) + jnp.log(1.0 + jnp.exp(-jnp.abs(z))))

    def suffix_sums(lg, carry):
        l_hi, l_lo = _split(lg)
        return _dot(tri, l_hi) + _dot(tri, l_lo) + carry

    acc_ref[...] = jnp.zeros_like(acc_ref)
    k_diag = keys(qi)
    k_nxt = keys(jnp.maximum(qi - 1, 0))
    causal = row < col
    carry = []
    for x in range(2):
        z = _dot_nt(k_diag, qh[x])
        c = suffix_sums(jnp.where(causal, log_complement(z), 0.0), 0.0)
        a_ref[x] = jnp.where(causal, jnp.exp(z + c), 0.0).astype(BF16)
        z_ref[x] = _dot_nt(k_nxt, qh[x])
        carry.append(c[0:1, :])

    def tile(it, carry):
        j = qi - it
        vt_prev = vt_ref[0, 0, j + 1]
        k_nxt = keys(jnp.maximum(j - 1, 0))
        new = []
        for x in range(2):
            acc_ref[x] += _dot(vt_prev, a_ref[x])
            z = z_ref[x]
            c = suffix_sums(log_complement(z), carry[x])
            a_ref[x] = jnp.exp(z + c).astype(BF16)
            z_ref[x] = _dot_nt(k_nxt, qh[x])
            new.append(c[0:1, :])
        return tuple(new)

    lax.fori_loop(1, qi + 1, tile, tuple(carry))
    vt_last = vt_ref[0, 0, 0]
    r = _iota((2 * HEAD_DIM, t), 0)
    o_t = jnp.where(r < HEAD_DIM, acc_ref[0] + _dot(vt_last, a_ref[0]), acc_ref[1] + _dot(vt_last, a_ref[1]))
    o_ref[0] = o_t.T.astype(o_ref.dtype)


def sb_attention(q, k, v):
    b, s, d = q.shape
    t = ATT_T
    pw = 2 * HEAD_DIM
    r = jnp.arange(t)
    tri = (r[None, :] >= r[:, None]).astype(BF16)
    q_spec, k_spec, vt_spec = _pair_specs(s, t)
    return pl.pallas_call(
        _sb_kernel, grid=(b, N_HEADS // 2, s // t),
        in_specs=[q_spec, k_spec, vt_spec, pl.BlockSpec((t, t), lambda i, h, qq: (0, 0))],
        out_specs=q_spec,
        out_shape=jax.ShapeDtypeStruct((b, s, d), BF16),
        scratch_shapes=[pltpu.VMEM((2, pw, t), F32), pltpu.VMEM((2, t, t), BF16), pltpu.VMEM((2, t, t), F32)],
        compiler_params=_cparams("parallel", "parallel", "arbitrary"), name="sb_attention",
    )(q, k, _pair_value_tiles(v, t), tri)


def _router_kernel(x_ref, g_ref, r_ref, o_ref):
    h = _rms(x_ref[...], g_ref[...])
    h_hi, h_lo = _split(h)
    r_hi, r_lo = _split(r_ref[...])
    logits = _dot(h_hi, r_hi) + _dot(h_hi, r_lo) + _dot(h_lo, r_hi)
    lane = _iota(logits.shape, 1)
    lane_f = lane.astype(F32)
    x = jnp.where(lane < N_EXPERTS, logits, NEG)
    m1, i1 = _first_argmax(x, lane_f, 1)
    m2, i2 = _first_argmax(jnp.where(lane_f == i1, PICKED, x), lane_f, 1)
    e = jnp.exp(m2 - m1)
    w1 = 1.0 / (1.0 + e)
    w2 = e * w1
    o_ref[...] = jnp.where(lane == 0, i1, jnp.where(lane == 1, i2, jnp.where(lane == 2, w1, jnp.where(lane == 3, w2, 0.0))))


def moe_router(x2, g, router):
    n_tok, d = x2.shape
    tm = min(PROJ_TM, n_tok)
    r_pad = jnp.pad(router, ((0, 0), (0, LANES - router.shape[1])))
    return pl.pallas_call(
        _router_kernel, grid=(n_tok // tm,),
        in_specs=[pl.BlockSpec((tm, d), lambda i: (i, 0)), pl.BlockSpec((1, d), lambda i: (0, 0)),
                  pl.BlockSpec((d, LANES), lambda i: (0, 0))],
        out_specs=pl.BlockSpec((tm, LANES), lambda i: (i, 0)),
        out_shape=jax.ShapeDtypeStruct((n_tok, LANES), F32),
        compiler_params=_cparams("parallel"), name="moe_router",
    )(x2, g.reshape(1, d), r_pad)


def _row_copy(src_ref, dst_ref, src_row, dst_row, sem):
    return pltpu.make_async_copy(src_ref.at[pl.ds(src_row, 1)], dst_ref.at[pl.ds(dst_row, 1)], sem)


def _gather_kernel(idx_ref, src_ref, o_ref, sem):
    rows = o_ref.shape[0]

    def start(r, c):
        _row_copy(src_ref, o_ref, idx_ref[0, 0, r], r, sem).start()
        return c

    def wait(r, c):
        _row_copy(src_ref, o_ref, 0, r, sem).wait()
        return c

    lax.fori_loop(0, rows, start, 0)
    lax.fori_loop(0, rows, wait, 0)


def gather_rows(src, idx):
    n = idx.shape[0]
    d = src.shape[1]
    r = DISPATCH_R if n % DISPATCH_R == 0 else MOE_TM
    return pl.pallas_call(
        _gather_kernel, grid=(n // r,),
        in_specs=[pl.BlockSpec((1, 1, r), lambda i: (i, 0, 0), memory_space=pltpu.SMEM),
                  pl.BlockSpec(memory_space=pl.ANY)],
        out_specs=pl.BlockSpec((r, d), lambda i: (i, 0)),
        out_shape=jax.ShapeDtypeStruct((n, d), src.dtype),
        scratch_shapes=[pltpu.SemaphoreType.DMA(())],
        compiler_params=_cparams("arbitrary"), name="moe_dispatch",
    )(idx.reshape(n // r, 1, r), src)


def _moe_ffn_kernel(te_ref, nu_ref, x_ref, ws_ref, g_ref, wg_ref, wu_ref, wd_ref, o_ref, h_ref, acc_ref):
    i = pl.program_id(0)
    j = pl.program_id(1)
    used = i < nu_ref[0]

    @pl.when(j == 0)
    def _():
        h_ref[...] = _rms(x_ref[...], g_ref[...]).astype(BF16)
        acc_ref[...] = jnp.zeros_like(acc_ref)

    @pl.when(used)
    def _():
        h = h_ref[...]
        a = _dot(h, wg_ref[0])
        u = _dot(h, wu_ref[0])
        act = (a * _sigmoid(a) * u).astype(BF16)
        acc_ref[...] += _dot(act, wd_ref[0])

    @pl.when(j == pl.num_programs(1) - 1)
    def _():
        o_ref[...] = acc_ref[...] * ws_ref[...]


def moe_ffn(x_sorted, w_slot, g, wg, wu, wd, tile_expert, n_used):
    n_slots, d = x_sorted.shape
    f = wg.shape[2]
    tm, tf = MOE_TM, FFN_TF
    nf = f // tf

    def fidx(i, j, nu):
        return jnp.where(i < nu[0], j, nf - 1)

    grid_spec = pltpu.PrefetchScalarGridSpec(
        num_scalar_prefetch=2, grid=(n_slots // tm, nf),
        in_specs=[pl.BlockSpec((tm, d), lambda i, j, te, nu: (i, 0)),
                  pl.BlockSpec((tm, 1), lambda i, j, te, nu: (i, 0)),
                  pl.BlockSpec((1, d), lambda i, j, te, nu: (0, 0)),
                  pl.BlockSpec((1, d, tf), lambda i, j, te, nu: (te[i], 0, fidx(i, j, nu))),
                  pl.BlockSpec((1, d, tf), lambda i, j, te, nu: (te[i], 0, fidx(i, j, nu))),
                  pl.BlockSpec((1, tf, d), lambda i, j, te, nu: (te[i], fidx(i, j, nu), 0))],
        out_specs=pl.BlockSpec((tm, d), lambda i, j, te, nu: (i, 0)),
        scratch_shapes=[pltpu.VMEM((tm, d), BF16), pltpu.VMEM((tm, d), F32)])
    return pl.pallas_call(
        _moe_ffn_kernel, grid_spec=grid_spec,
        out_shape=jax.ShapeDtypeStruct((n_slots, d), F32),
        compiler_params=_cparams("arbitrary", "arbitrary"), name="moe_ffn",
    )(tile_expert, n_used, x_sorted, w_slot, g.reshape(1, d), wg, wu, wd)


def _combine_kernel(s1_ref, s2_ref, x_ref, y_ref, g_ref, o_ref, b1_ref, b2_ref, sem, *, final_norm):
    rows = o_ref.shape[0]

    def start(r, c):
        _row_copy(y_ref, b1_ref, s1_ref[0, 0, r], r, sem).start()
        _row_copy(y_ref, b2_ref, s2_ref[0, 0, r], r, sem).start()
        return c

    def wait(r, c):
        _row_copy(y_ref, b1_ref, 0, r, sem).wait()
        _row_copy(y_ref, b2_ref, 0, r, sem).wait()
        return c

    lax.fori_loop(0, rows, start, 0)
    lax.fori_loop(0, rows, wait, 0)
    out = x_ref[...] + b1_ref[...] + b2_ref[...]
    if final_norm:
        out = _rms(out, g_ref[...])
    o_ref[...] = out


def moe_combine(x2, y, slot1, slot2, final_g):
    n_tok, d = x2.shape
    r = min(COMBINE_R, n_tok)
    final_norm = final_g is not None
    g = final_g if final_norm else jnp.ones((d,), F32)
    idx_spec = pl.BlockSpec((1, 1, r), lambda i: (i, 0, 0), memory_space=pltpu.SMEM)
    return pl.pallas_call(
        functools.partial(_combine_kernel, final_norm=final_norm), grid=(n_tok // r,),
        in_specs=[idx_spec, idx_spec, pl.BlockSpec((r, d), lambda i: (i, 0)),
                  pl.BlockSpec(memory_space=pl.ANY), pl.BlockSpec((1, d), lambda i: (0, 0))],
        out_specs=pl.BlockSpec((r, d), lambda i: (i, 0)),
        out_shape=jax.ShapeDtypeStruct((n_tok, d), F32),
        scratch_shapes=[pltpu.VMEM((r, d), F32), pltpu.VMEM((r, d), F32), pltpu.SemaphoreType.DMA(())],
        compiler_params=_cparams("arbitrary"), name="moe_combine",
    )(slot1.reshape(n_tok // r, 1, r), slot2.reshape(n_tok // r, 1, r), x2, y, g.reshape(1, d))


def moe_layer(x2, g, router, wg, wu, wd, final_g=None):
    n_tok, d = x2.shape
    tm = MOE_TM
    n_e = router.shape[1]
    info = moe_router(x2, g, router)
    expert = info[:, :2].astype(jnp.int32).reshape(-1)
    weight = info[:, 2:4].reshape(-1)
    onehot = (expert[:, None] == jnp.arange(n_e)[None, :]).astype(jnp.int32)
    csum = jnp.cumsum(onehot, axis=0)
    rank = jnp.take_along_axis(csum, expert[:, None], axis=1)[:, 0] - 1
    padded = ((csum[-1] + tm - 1) // tm) * tm
    ends = jnp.cumsum(padded)
    slot = (ends - padded)[expert] + rank
    n_slots = 2 * n_tok + n_e * tm
    n_tiles = n_slots // tm
    token_of_slot = jnp.zeros((n_slots,), jnp.int32).at[slot].set(jnp.arange(2 * n_tok, dtype=jnp.int32) // 2)
    weight_of_slot = jnp.zeros((n_slots,), F32).at[slot].set(weight)
    n_used = (ends[-1] // tm).astype(jnp.int32)
    tile_start = jnp.minimum(jnp.arange(n_tiles, dtype=jnp.int32), n_used - 1) * tm
    tile_expert = jnp.minimum(jnp.sum((ends[None, :] <= tile_start[:, None]).astype(jnp.int32), axis=1), n_e - 1)

    x_sorted = gather_rows(x2, token_of_slot)
    y = moe_ffn(x_sorted, weight_of_slot.reshape(n_slots, 1), g, wg, wu, wd, tile_expert, n_used.reshape(1))
    slot2 = slot.reshape(n_tok, 2)
    return moe_combine(x2, y, slot2[:, 0], slot2[:, 1], final_g)


def _alibi_slopes():
    return 2.0 ** (-8.0 * jnp.arange(1, N_HEADS + 1, dtype=F32) / N_HEADS)


def nsa_layer(x, g, w_in, cmpk, cmpv, w_out):
    b, s, d = x.shape
    n_tok = b * s
    x2 = x.reshape(n_tok, d)
    grp, hpg, dh, t = NSA_GROUPS, NSA_HPG, HEAD_DIM, ATT_T
    kvd = grp * dh
    w_q = w_in[:, :d].astype(BF16)
    w_kv = w_in[:, d:d + 6 * kvd].astype(BF16)
    w_g = jnp.pad(w_in[:, d + 6 * kvd:], ((0, 0), (0, LANES - 3 * N_HEADS))).astype(BF16)
    q, kv, gl = norm_proj(x2, g, [w_q, w_kv, w_g], [BF16, BF16, F32])
    kc, vc, ks, vs, kw, vw = (kv[:, i * kvd:(i + 1) * kvd] for i in range(6))

    def chunks(a):
        a = a.reshape(b, s // CMP_STRIDE, CMP_STRIDE, grp, dh).transpose(0, 3, 1, 2, 4)
        return a.reshape(b, grp, s // CMP_STRIDE, CMP_STRIDE * dh)

    def rep(a):
        a = jnp.broadcast_to(a.reshape(b, s, grp, 1, dh), (b, s, grp, hpg, dh))
        return a.reshape(b, s, grp * hpg * dh)

    def tiles_t(a):
        return a.reshape(b, s // t, t, grp, dh).transpose(0, 3, 1, 4, 2)

    k_cmp = nsa_compress(chunks(kc), *cmpk, transposed=False)
    v_cmp_t = nsa_compress(chunks(vc), *cmpv, transposed=True)
    glt = gl[:, :3 * N_HEADS].reshape(b, s, 3, grp, hpg).transpose(0, 3, 2, 4, 1).reshape(b, grp, 3 * hpg, s)
    glt = jnp.pad(glt, ((0, 0), (0, 0), (0, N_GATE_ROWS - 3 * hpg), (0, 0)))
    o = nsa_attention(q.reshape(b, s, d), glt, k_cmp, v_cmp_t, rep(ks), tiles_t(vs), rep(kw), tiles_t(vw),
                      _alibi_slopes())
    return matmul_residual(o.reshape(n_tok, d), w_out.astype(BF16), x2).reshape(b, s, d)


def _qkv_layer(x, g, w_in, w_out, attend):
    b, s, d = x.shape
    n_tok = b * s
    x2 = x.reshape(n_tok, d)
    ws = [w_in[:, i * d:(i + 1) * d].astype(BF16) for i in range(3)]
    q, k, v = (a.reshape(b, s, d) for a in norm_proj(x2, g, ws, [BF16] * 3))
    o = attend(q, k, v)
    return matmul_residual(o.reshape(n_tok, d), w_out.astype(BF16), x2).reshape(b, s, d)


def moba_layer(x, g, w_in, w_out):
    slopes = _alibi_slopes()
    return _qkv_layer(x, g, w_in, w_out, lambda q, k, v: moba_attention(q, k, v, slopes))


def sb_layer(x, g, w_in, w_out):
    return _qkv_layer(x, g, w_in, w_out, sb_attention)


def dense_ffn_layer(x, g, wg, wu, wd):
    b, s, d = x.shape
    return dense_ffn(x.reshape(b * s, d), g, wg.astype(BF16), wu.astype(BF16), wd.astype(BF16)).reshape(b, s, d)


def moe_ffn_layer(x, g, router, wg, wu, wd, final_g=None):
    b, s, d = x.shape
    return moe_layer(x.reshape(b * s, d), g, router, wg.astype(BF16), wu.astype(BF16), wd.astype(BF16),
                     final_g).reshape(b, s, d)


def kernel(x, l0_attn_norm, l0_nsa_w_in, l0_nsa_cmpk_pos, l0_nsa_cmpk_w1, l0_nsa_cmpk_w2, l0_nsa_cmpv_pos, l0_nsa_cmpv_w1, l0_nsa_cmpv_w2, l0_nsa_w_out, l0_ffn_norm, l0_ffn_w_gate, l0_ffn_w_up, l0_ffn_w_down, l1_attn_norm, l1_moba_w_in, l1_moba_w_out, l1_ffn_norm, l1_moe_router, l1_moe_w_gate, l1_moe_w_up, l1_moe_w_down, l2_attn_norm, l2_sb_w_in, l2_sb_w_out, l2_ffn_norm, l2_ffn_w_gate, l2_ffn_w_up, l2_ffn_w_down, l3_attn_norm, l3_nsa_w_in, l3_nsa_cmpk_pos, l3_nsa_cmpk_w1, l3_nsa_cmpk_w2, l3_nsa_cmpv_pos, l3_nsa_cmpv_w1, l3_nsa_cmpv_w2, l3_nsa_w_out, l3_ffn_norm, l3_moe_router, l3_moe_w_gate, l3_moe_w_up, l3_moe_w_down, final_norm):
    x = nsa_layer(x, l0_attn_norm, l0_nsa_w_in, (l0_nsa_cmpk_pos, l0_nsa_cmpk_w1, l0_nsa_cmpk_w2),
                  (l0_nsa_cmpv_pos, l0_nsa_cmpv_w1, l0_nsa_cmpv_w2), l0_nsa_w_out)
    x = dense_ffn_layer(x, l0_ffn_norm, l0_ffn_w_gate, l0_ffn_w_up, l0_ffn_w_down)
    x = moba_layer(x, l1_attn_norm, l1_moba_w_in, l1_moba_w_out)
    x = moe_ffn_layer(x, l1_ffn_norm, l1_moe_router, l1_moe_w_gate, l1_moe_w_up, l1_moe_w_down)
    x = sb_layer(x, l2_attn_norm, l2_sb_w_in, l2_sb_w_out)
    x = dense_ffn_layer(x, l2_ffn_norm, l2_ffn_w_gate, l2_ffn_w_up, l2_ffn_w_down)
    x = nsa_layer(x, l3_attn_norm, l3_nsa_w_in, (l3_nsa_cmpk_pos, l3_nsa_cmpk_w1, l3_nsa_cmpk_w2),
                  (l3_nsa_cmpv_pos, l3_nsa_cmpv_w1, l3_nsa_cmpv_w2), l3_nsa_w_out)
    return moe_ffn_layer(x, l3_ffn_norm, l3_moe_router, l3_moe_w_gate, l3_moe_w_up, l3_moe_w_down, final_norm)
```
